```python
import jax, jax.numpy as jnp
from jax import lax
import numpy as np

D_MODEL = 4096
BATCH = 1
SEQ = 8192
DEPTH = 1
DEC_BATCH = 128
DEC_SEQ = 1
PAST_LEN = 8192
PAGE_SIZE = 128

D_RNN = D_MODEL
N_RG_BLOCKS = 16
RG_BLOCK = D_RNN // N_RG_BLOCKS
CONV_W = 4
RG_C = 8.0
HEAD_DIM = 128
N_Q_HEADS = D_MODEL // HEAD_DIM
N_KV_HEADS = 8
GQA_GROUP = N_Q_HEADS // N_KV_HEADS
WINDOW = 128
ATTN_BLOCK = WINDOW
Q_WIDTH = N_Q_HEADS * HEAD_DIM
KV_WIDTH = N_KV_HEADS * HEAD_DIM
D_FF = 11008
MACARON_WEIGHT = 0.5
NORM_EPS = 1e-6
IN_SPLITS = (D_RNN, 2 * D_RNN, 2 * D_RNN + Q_WIDTH, 2 * D_RNN + Q_WIDTH + KV_WIDTH,
             2 * D_RNN + Q_WIDTH + 2 * KV_WIDTH, 2 * D_RNN + Q_WIDTH + 2 * KV_WIDTH + D_MODEL)
IN_COLS = IN_SPLITS[-1] + D_MODEL

kernel_name = "griffin_swa_sink_macaron_step"


def _rms(x, g):
    xf = x.astype(jnp.float32)
    y = xf * lax.rsqrt(jnp.mean(xf * xf, axis=-1, keepdims=True) + NORM_EPS)
    return (y * g.astype(jnp.float32)).astype(x.dtype)


def _swiglu(x, wg, wu, wd):
    return (jax.nn.silu(x @ wg) * (x @ wu)) @ wd


def _block_diag(x, w, b):
    xb = x.reshape(x.shape[:-1] + (N_RG_BLOCKS, RG_BLOCK))
    y = jnp.einsum('btni,nij->btnj', xb, w) + b
    return y.reshape(x.shape)


def _causal_conv(x, prefix, w, b):
    xp = jnp.concatenate([prefix, x], axis=1)
    T = x.shape[1]
    y = b
    for j in range(CONV_W):
        y = y + w[j] * xp[:, j:j + T]
    return y, xp[:, -(CONV_W - 1):]


def _lin_combine(e1, e2):
    a1, b1 = e1
    a2, b2 = e2
    return a1 * a2, a2 * b1 + b2


def _rg_lru(x, pos, h0, w_a, b_a, w_x, b_x, lam):
    xf = x.astype(jnp.float32)
    r = jax.nn.sigmoid(_block_diag(x, w_a, b_a).astype(jnp.float32))
    i = jax.nn.sigmoid(_block_diag(x, w_x, b_x).astype(jnp.float32))
    log_a = RG_C * r * jax.nn.log_sigmoid(lam.astype(jnp.float32))
    a = jnp.exp(log_a)
    mult = jnp.where((pos == 0)[None, :, None], 1.0, jnp.sqrt(-jnp.expm1(2.0 * log_a)))
    u = mult * i * xf
    u = u.at[:, 0].add(a[:, 0] * h0.astype(jnp.float32))
    _, h = lax.associative_scan(_lin_combine, (a, u), axis=1)
    return h, h[:, -1]


def _alibi_slopes():
    return jnp.exp2(-8.0 * jnp.arange(1, N_Q_HEADS + 1, dtype=jnp.float32) / N_Q_HEADS)


def _attend_band(q, k, v, dist, valid, sinks):
    s = jnp.einsum('bnqkgd,bnskd->bnkgqs', q, k).astype(jnp.float32) * (HEAD_DIM ** -0.5)
    slopes = _alibi_slopes().reshape(N_KV_HEADS, GQA_GROUP, 1, 1)
    s = s - slopes * dist.astype(jnp.float32)
    s = jnp.where(valid[None, :, None, None], s, -jnp.inf)
    sink = sinks.astype(jnp.float32).reshape(N_KV_HEADS, GQA_GROUP, 1, 1)
    m = jnp.maximum(jnp.max(s, axis=-1, keepdims=True), sink)
    p = jnp.exp(s - m)
    p = p / (jnp.sum(p, axis=-1, keepdims=True) + jnp.exp(sink - m))
    return jnp.einsum('bnkgqs,bnskd->bnqkgd', p.astype(v.dtype), v)


def _prompt_attention(q, k, v, sinks):
    b, T = q.shape[:2]
    n = T // ATTN_BLOCK
    qb = q.reshape(b, n, ATTN_BLOCK, N_KV_HEADS, GQA_GROUP, HEAD_DIM)

    def band(z):
        zb = z.reshape(b, n, ATTN_BLOCK, N_KV_HEADS, HEAD_DIM)
        prev = jnp.concatenate([jnp.zeros_like(zb[:, :1]), zb[:, :-1]], axis=1)
        return jnp.concatenate([prev, zb], axis=2)

    qi = jnp.arange(ATTN_BLOCK)[:, None]
    kj = jnp.arange(2 * ATTN_BLOCK)[None, :]
    dist = qi + ATTN_BLOCK - kj
    band_ok = (dist >= 0) & (dist < WINDOW)
    has_prev = (jnp.arange(n) > 0)[:, None, None] | (kj >= ATTN_BLOCK)[None]
    valid = band_ok[None] & has_prev
    o = _attend_band(qb, band(k), band(v), dist, valid, sinks)
    return o.reshape(b, T, Q_WIDTH), k[:, -WINDOW:], v[:, -WINDOW:]


def _sample_attention(q, k, v, k_buf, v_buf, sinks):
    b, T = q.shape[:2]
    kk = jnp.concatenate([k_buf.astype(k.dtype), k], axis=1)
    vv = jnp.concatenate([v_buf.astype(v.dtype), v], axis=1)
    dist = jnp.arange(T)[:, None] + WINDOW - jnp.arange(WINDOW + T)[None, :]
    valid = ((dist >= 0) & (dist < WINDOW))[None]
    qb = q.reshape(b, 1, T, N_KV_HEADS, GQA_GROUP, HEAD_DIM)
    o = _attend_band(qb, kk[:, None], vv[:, None], dist, valid, sinks)
    return o.reshape(b, T, Q_WIDTH), kk[:, -WINDOW:], vv[:, -WINDOW:]


def _layer(x, pos, conv_buf, h0, k_buf, v_buf, p):
    b, T = x.shape[:2]
    x = x + MACARON_WEIGHT * _swiglu(_rms(x, p['norm_ffn1']), p['w_ffn1_gate'], p['w_ffn1_up'], p['w_ffn1_down'])
    xn = _rms(x, p['norm_mix'])
    xr, yr, q, k, v, g_rec, g_att = jnp.split(xn @ p['w_in'], IN_SPLITS, axis=-1)
    xc, conv_new = _causal_conv(xr, conv_buf.astype(x.dtype), p['conv_w'], p['conv_b'])
    h, h_last = _rg_lru(xc, pos, h0, p['rg_w_a'], p['rg_b_a'], p['rg_w_x'], p['rg_b_x'], p['rg_lambda'])
    y_rec = (h.astype(x.dtype) * jax.nn.gelu(yr)) @ p['w_lru_proj']
    q = _rms(q.reshape(b, T, N_Q_HEADS, HEAD_DIM), p['q_norm'])
    k = _rms(k.reshape(b, T, N_KV_HEADS, HEAD_DIM), p['k_norm'])
    v = v.reshape(b, T, N_KV_HEADS, HEAD_DIM)
    if k_buf is None:
        o, k_new, v_new = _prompt_attention(q, k, v, p['sinks'])
    else:
        o, k_new, v_new = _sample_attention(q, k, v, k_buf, v_buf, p['sinks'])
    y_att = o @ p['w_attn_proj']
    mixed = (jax.nn.sigmoid(g_rec) * y_rec + jax.nn.sigmoid(g_att) * y_att) @ p['w_out']
    x = x + mixed
    x = x + MACARON_WEIGHT * _swiglu(_rms(x, p['norm_ffn2']), p['w_ffn2_gate'], p['w_ffn2_up'], p['w_ffn2_down'])
    return x, conv_new, h_last.astype(x.dtype), k_new, v_new


def setup_inputs(seed: int = 0) -> dict:
    key = jax.random.key(seed)
    ks = iter(jax.random.split(key, 40))
    f32 = jnp.float32

    def nrm(shape, scale):
        return jax.random.normal(next(ks), shape, f32) * scale

    def gain(shape):
        return 1.0 + nrm(shape, 0.02)

    u = jnp.sqrt(jax.random.uniform(next(ks), (DEPTH, D_RNN), f32, 0.81, 0.998))
    rg_lambda = jnp.log(u) - jnp.log1p(-u)
    return {
        'x_prompt': nrm((BATCH, SEQ, D_MODEL), 1.0),
        'x_sample': nrm((DEC_BATCH, DEC_SEQ, D_MODEL), 1.0),
        'state_conv': nrm((DEPTH, DEC_BATCH, CONV_W - 1, D_RNN), 1.0),
        'state_h': nrm((DEPTH, DEC_BATCH, D_RNN), 0.5),
        'cache_k': nrm((DEPTH, DEC_BATCH, WINDOW, N_KV_HEADS, HEAD_DIM), 1.0),
        'cache_v': nrm((DEPTH, DEC_BATCH, WINDOW, N_KV_HEADS, HEAD_DIM), 1.0),
        'norm_ffn1': gain((DEPTH, D_MODEL)),
        'w_ffn1_gate': nrm((DEPTH, D_MODEL, D_FF), D_MODEL ** -0.5),
        'w_ffn1_up': nrm((DEPTH, D_MODEL, D_FF), D_MODEL ** -0.5),
        'w_ffn1_down': nrm((DEPTH, D_FF, D_MODEL), D_FF ** -0.5),
        'norm_mix': gain((DEPTH, D_MODEL)),
        'w_in': nrm((DEPTH, D_MODEL, IN_COLS), D_MODEL ** -0.5),
        'conv_w': nrm((DEPTH, CONV_W, D_RNN), CONV_W ** -0.5),
        'conv_b': nrm((DEPTH, D_RNN), 0.01),
        'rg_w_a': nrm((DEPTH, N_RG_BLOCKS, RG_BLOCK, RG_BLOCK), RG_BLOCK ** -0.5),
        'rg_b_a': nrm((DEPTH, N_RG_BLOCKS, RG_BLOCK), 0.01),
        'rg_w_x': nrm((DEPTH, N_RG_BLOCKS, RG_BLOCK, RG_BLOCK), RG_BLOCK ** -0.5),
        'rg_b_x': nrm((DEPTH, N_RG_BLOCKS, RG_BLOCK), 0.01),
        'rg_lambda': rg_lambda,
        'q_norm': gain((DEPTH, HEAD_DIM)),
        'k_norm': gain((DEPTH, HEAD_DIM)),
        'sinks': nrm((DEPTH, N_Q_HEADS), 0.5),
        'w_lru_proj': nrm((DEPTH, D_RNN, D_MODEL), D_RNN ** -0.5),
        'w_attn_proj': nrm((DEPTH, Q_WIDTH, D_MODEL), Q_WIDTH ** -0.5),
        'w_out': nrm((DEPTH, D_MODEL, D_MODEL), D_MODEL ** -0.5),
        'norm_ffn2': gain((DEPTH, D_MODEL)),
        'w_ffn2_gate': nrm((DEPTH, D_MODEL, D_FF), D_MODEL ** -0.5),
        'w_ffn2_up': nrm((DEPTH, D_MODEL, D_FF), D_MODEL ** -0.5),
        'w_ffn2_down': nrm((DEPTH, D_FF, D_MODEL), D_FF ** -0.5),
    }


def reference(x_prompt, x_sample, state_conv, state_h, cache_k, cache_v,
              norm_ffn1, w_ffn1_gate, w_ffn1_up, w_ffn1_down, norm_mix, w_in,
              conv_w, conv_b, rg_w_a, rg_b_a, rg_w_x, rg_b_x, rg_lambda,
              q_norm, k_norm, sinks, w_lru_proj, w_attn_proj, w_out,
              norm_ffn2, w_ffn2_gate, w_ffn2_up, w_ffn2_down):
    b_p, t_p = x_prompt.shape[:2]
    t_s = x_sample.shape[1]
    pos_p = jnp.arange(t_p)
    pos_s = PAST_LEN + jnp.arange(t_s)
    yp, ys = x_prompt, x_sample
    pc, ph, pk, pv, sc, sh, sk, sv = [], [], [], [], [], [], [], []
    for l in range(DEPTH):
        p = dict(norm_ffn1=norm_ffn1[l], w_ffn1_gate=w_ffn1_gate[l], w_ffn1_up=w_ffn1_up[l],
                 w_ffn1_down=w_ffn1_down[l], norm_mix=norm_mix[l], w_in=w_in[l],
                 conv_w=conv_w[l], conv_b=conv_b[l], rg_w_a=rg_w_a[l], rg_b_a=rg_b_a[l],
                 rg_w_x=rg_w_x[l], rg_b_x=rg_b_x[l], rg_lambda=rg_lambda[l],
                 q_norm=q_norm[l], k_norm=k_norm[l], sinks=sinks[l],
                 w_lru_proj=w_lru_proj[l], w_attn_proj=w_attn_proj[l], w_out=w_out[l],
                 norm_ffn2=norm_ffn2[l], w_ffn2_gate=w_ffn2_gate[l], w_ffn2_up=w_ffn2_up[l],
                 w_ffn2_down=w_ffn2_down[l])
        conv0 = jnp.zeros((b_p, CONV_W - 1, D_RNN), yp.dtype)
        h0 = jnp.zeros((b_p, D_RNN), jnp.float32)
        yp, c1, h1, k1, v1 = _layer(yp, pos_p, conv0, h0, None, None, p)
        ys, c2, h2, k2, v2 = _layer(ys, pos_s, state_conv[l], state_h[l], cache_k[l], cache_v[l], p)
        pc.append(c1); ph.append(h1); pk.append(k1); pv.append(v1)
        sc.append(c2); sh.append(h2); sk.append(k2); sv.append(v2)
    return (yp, ys, jnp.stack(pc), jnp.stack(ph), jnp.stack(pk), jnp.stack(pv),
            jnp.stack(sc), jnp.stack(sh), jnp.stack(sk), jnp.stack(sv))
```

```python
import functools
import math

import jax
import jax.numpy as jnp
from jax import lax
from jax.experimental import pallas as pl
from jax.experimental.pallas import tpu as pltpu

HEAD_DIM = 128
N_KV_HEADS = 8
WINDOW = 128
N_RG_BLOCKS = 16
CONV_W = 4
RG_C = 8.0
NORM_EPS = 1e-6
MACARON_WEIGHT = 0.5

V7X_VMEM_BYTES = 64 * 1024 * 1024
V7X_LANES = 128
V7X_SUBLANES = 8
V7X_MXU_DIM = 256
V7X_VMEM_RESERVE_BYTES = 4 * 1024 * 1024

F32 = jnp.float32
BF16 = jnp.bfloat16


def _nbytes(shape, dtype):
    return math.prod(shape) * jnp.dtype(dtype).itemsize


def _vmem_limit(pipelined_bytes, resident_bytes):
    want = 2 * pipelined_bytes + resident_bytes + V7X_VMEM_RESERVE_BYTES
    return int(min(want, V7X_VMEM_BYTES - 1024 * 1024))


def _params(n_grid, pipelined_bytes, resident_bytes):
    return pltpu.CompilerParams(
        dimension_semantics=("arbitrary",) * n_grid,
        vmem_limit_bytes=_vmem_limit(pipelined_bytes, resident_bytes))


def _shift_div(x, n):
    assert n & (n - 1) == 0
    return lax.shift_right_logical(x, jnp.int32(n.bit_length() - 1))


def _rms_kernel(x_ref, g_ref, o_ref):
    x = x_ref[...]
    ms = jnp.mean(x * x, axis=-1, keepdims=True)
    o_ref[...] = (x * lax.rsqrt(ms + NORM_EPS) * g_ref[...]).astype(o_ref.dtype)


def _rmsnorm(x, g):
    rows, d = x.shape
    rb = min(rows, 256)
    assert rows % rb == 0
    blk = _nbytes((rb, d), F32) + _nbytes((rb, d), BF16)
    return pl.pallas_call(
        _rms_kernel,
        grid=(rows // rb,),
        in_specs=[pl.BlockSpec((rb, d), lambda i: (i, 0)),
                  pl.BlockSpec((1, d), lambda i: (0, 0))],
        out_specs=pl.BlockSpec((rb, d), lambda i: (i, 0)),
        out_shape=jax.ShapeDtypeStruct((rows, d), BF16),
        compiler_params=_params(1, blk, 3 * _nbytes((rb, d), F32)),
        name="rmsnorm",
    )(x, g.reshape(1, d))


def _tile_maps(n_i, n_j):
    def prompt(i, j, *_):
        return (jnp.minimum(i, n_i - 2), jnp.where(i == n_i - 1, n_j - 1, j))

    def sample(i, j, *_):
        return (0, jnp.where(i == n_i - 1, j, 0))

    return prompt, sample


def _mm_kernel(*refs, n_x, n_w, extra_kinds, n_out, epilogue, n_i):
    pos = 0
    xs = [(refs[pos + 2 * t], refs[pos + 2 * t + 1]) for t in range(n_x)]
    pos += 2 * n_x
    ws = list(refs[pos:pos + n_w])
    pos += n_w
    extras = []
    for kind in extra_kinds:
        if kind == "pair":
            extras.append((refs[pos], refs[pos + 1]))
            pos += 2
        else:
            extras.append((refs[pos], refs[pos]))
            pos += 1
    outs = [(refs[pos + 2 * t], refs[pos + 2 * t + 1]) for t in range(n_out)]
    i = pl.program_id(0)

    def compute(which):
        accs = []
        for t in range(n_w):
            x = xs[t if n_x > 1 else 0][which][...]
            w = ws[t][...].astype(BF16)
            accs.append(jnp.dot(x, w, preferred_element_type=F32))
        res = epilogue(accs, [e[which][...] for e in extras])
        for o, r in zip(outs, res):
            o[which][...] = r.astype(o[which].dtype)

    @pl.when(i < n_i - 1)
    def _():
        compute(0)

    @pl.when(i == n_i - 1)
    def _():
        compute(1)


def _matmul(name, xs, ws, extras, out_dtypes, epilogue, *, n_cols, tm, tn):
    t_p, k = xs[0][0].shape
    t_s = xs[0][1].shape[0]
    assert t_p % tm == 0 and n_cols % tn == 0
    n_i = t_p // tm + 1
    n_j = n_cols // tn
    map_p, map_s = _tile_maps(n_i, n_j)

    in_specs, args = [], []
    pipelined = 0
    for x_p, x_s in xs:
        in_specs += [pl.BlockSpec((tm, k), lambda i, j: (jnp.minimum(i, n_i - 2), 0)),
                     pl.BlockSpec((t_s, k), lambda i, j: (0, 0))]
        args += [x_p, x_s]
        pipelined += _nbytes((tm + t_s, k), x_p.dtype)
    for w, col0 in ws:
        assert col0 % tn == 0 and w.shape[0] == k
        in_specs.append(pl.BlockSpec((k, tn), lambda i, j, c=col0 // tn: (0, j + c)))
        args.append(w)
        pipelined += _nbytes((k, tn), w.dtype)
    kinds = []
    for e in extras:
        kinds.append(e[0])
        if e[0] == "pair":
            in_specs += [pl.BlockSpec((tm, tn), map_p), pl.BlockSpec((t_s, tn), map_s)]
            args += [e[1], e[2]]
            pipelined += _nbytes((tm + t_s, tn), e[1].dtype)
        elif e[0] == "row":
            in_specs.append(pl.BlockSpec((1, tn), lambda i, j: (0, j)))
            args.append(e[1])
        else:
            assert e[0] == "const" and e[1].ndim == 2
            in_specs.append(pl.BlockSpec(e[1].shape, lambda i, j: (0, 0)))
            args.append(e[1])
    out_specs, out_shape = [], []
    for dt in out_dtypes:
        out_specs += [pl.BlockSpec((tm, tn), map_p), pl.BlockSpec((t_s, tn), map_s)]
        out_shape += [jax.ShapeDtypeStruct((t_p, n_cols), dt),
                      jax.ShapeDtypeStruct((t_s, n_cols), dt)]
        pipelined += _nbytes((tm + t_s, tn), dt)
    resident = len(ws) * (_nbytes((k, tn), BF16) + 2 * _nbytes((tm, tn), F32))
    res = pl.pallas_call(
        functools.partial(_mm_kernel, n_x=len(xs), n_w=len(ws), extra_kinds=tuple(kinds),
                          n_out=len(out_dtypes), epilogue=epilogue, n_i=n_i),
        grid=(n_i, n_j),
        in_specs=in_specs,
        out_specs=out_specs,
        out_shape=out_shape,
        compiler_params=_params(2, pipelined, resident),
        name=name,
    )(*args)
    return [(res[2 * t], res[2 * t + 1]) for t in range(len(out_dtypes))]


def _sigmoid(x):
    return 1.0 / (1.0 + jnp.exp(-x))


def _ep_identity(accs, extras):
    return [accs[0]]


def _ep_twice(accs, extras):
    return [accs[0], accs[0]]


def _ep_swiglu(accs, extras):
    g, u = accs
    return [g * _sigmoid(g) * u]


def _ep_gelu(accs, extras):
    x = accs[0]
    c = math.sqrt(2.0 / math.pi)
    return [0.5 * x * (1.0 + jnp.tanh(c * (x + 0.044715 * (x * x * x))))]


def _ep_sigmoid(accs, extras):
    return [_sigmoid(accs[0])]


def _head_rms(acc, gain):
    parts = []
    for h in range(acc.shape[1] // HEAD_DIM):
        y = acc[:, h * HEAD_DIM:(h + 1) * HEAD_DIM]
        ms = jnp.mean(y * y, axis=-1, keepdims=True)
        parts.append(y * lax.rsqrt(ms + NORM_EPS) * gain)
    return jnp.concatenate(parts, axis=1)


def _ep_head_rms(accs, extras):
    return [_head_rms(accs[0], extras[0])]


def _ep_head_rms_twice(accs, extras):
    y = _head_rms(accs[0], extras[0])
    return [y, y]


def _ep_gate_mul(accs, extras):
    return [extras[0] * accs[0]]


def _ep_gate_mul_add(accs, extras):
    return [extras[1] + extras[0] * accs[0]]


def _ep_residual(accs, extras):
    return [extras[0] + accs[0]]


def _down_kernel(a_p, a_s, w_ref, r_p, r_s, o_p, o_s, *, n_i, k_half):
    i = pl.program_id(0)
    kk = pl.program_id(2)

    def compute(a_ref, r_ref, o_ref):
        for half in range(2):
            @pl.when(kk == half)
            def _(half=half):
                a = a_ref[:, half * k_half:(half + 1) * k_half]
                w = w_ref[...].astype(BF16)
                d = MACARON_WEIGHT * jnp.dot(a, w, preferred_element_type=F32)
                if half == 0:
                    o_ref[...] = r_ref[...] + d
                else:
                    o_ref[...] += d

    @pl.when(i < n_i - 1)
    def _():
        compute(a_p, r_p, o_p)

    @pl.when(i == n_i - 1)
    def _():
        compute(a_s, r_s, o_s)


def _ffn_down(a, w, res, *, tm, tn):
    a_p, a_s = a
    r_p, r_s = res
    t_p, k = a_p.shape
    t_s = a_s.shape[0]
    n_cols = w.shape[1]
    assert k % 2 == 0 and (k // 2) % V7X_LANES == 0
    k_half = k // 2
    n_i = t_p // tm + 1
    n_j = n_cols // tn
    map_p, map_s = _tile_maps(n_i, n_j)
    tile_p = pl.BlockSpec((tm, tn), map_p)
    tile_s = pl.BlockSpec((t_s, tn), map_s)
    pipelined = (_nbytes((k_half, tn), F32) + 2 * _nbytes((tm + t_s, tn), F32)
                 + _nbytes((t_s, k), BF16))
    resident = (_nbytes((tm, k), BF16) + _nbytes((k_half, tn), BF16)
                + 2 * _nbytes((tm, tn), F32))
    o_p, o_s = pl.pallas_call(
        functools.partial(_down_kernel, n_i=n_i, k_half=k_half),
        grid=(n_i, n_j, 2),
        in_specs=[
            pl.BlockSpec((tm, k), lambda i, j, kk: (jnp.minimum(i, n_i - 2), 0),
                         pipeline_mode=pl.Buffered(1)),
            pl.BlockSpec((t_s, k), lambda i, j, kk: (0, 0)),
            pl.BlockSpec((k_half, tn), lambda i, j, kk: (kk, j)),
            tile_p, tile_s],
        out_specs=[tile_p, tile_s],
        out_shape=[jax.ShapeDtypeStruct((t_p, n_cols), F32),
                   jax.ShapeDtypeStruct((t_s, n_cols), F32)],
        compiler_params=_params(3, pipelined, resident),
        name="ffn_down",
    )(a_p, a_s, w, r_p, r_s)
    return o_p, o_s


def _log_sigmoid(x):
    return jnp.minimum(x, 0.0) - jnp.log1p(jnp.exp(-jnp.abs(x)))


def _rg_gate_block(xc, wa, wx, ba, bx, lam, first_pos):
    xb = xc.astype(BF16)
    r = _sigmoid(jnp.dot(xb, wa, preferred_element_type=F32) + ba)
    g = _sigmoid(jnp.dot(xb, wx, preferred_element_type=F32) + bx)
    log_a = RG_C * r * _log_sigmoid(lam)
    a = jnp.exp(log_a)
    mult = jnp.sqrt(1.0 - jnp.exp(2.0 * log_a))
    if first_pos is not None:
        mult = jnp.where(first_pos, 1.0, mult)
    return a, mult * g * xc


def _lru_prompt_kernel(xr_ref, gy_ref, cw_ref, cb_ref, wa_ref, wx_ref, ba_ref, bx_ref,
                       lam_ref, hg_ref, hlast_ref, xp_buf, xc_buf, a_buf, b_buf, h_carry,
                       *, tc):
    t = pl.program_id(0)
    d = xr_ref.shape[1]
    bw = d // N_RG_BLOCKS
    pad = V7X_SUBLANES

    @pl.when(t == 0)
    def _():
        xp_buf[0:pad, :] = jnp.zeros((pad, d), F32)
        h_carry[...] = jnp.zeros_like(h_carry)

    xp_buf[pad:pad + tc, :] = xr_ref[...]
    xc = cb_ref[...]
    for j in range(CONV_W):
        off = pad - (CONV_W - 1) + j
        xc = xc + cw_ref[j:j + 1, :] * xp_buf[off:off + tc, :]
    xc_buf[...] = xc
    xp_buf[0:pad, :] = xp_buf[tc:tc + pad, :]

    row = lax.broadcasted_iota(jnp.int32, (tc, 1), 0)
    first_pos = (row + t * tc) == 0
    sub = jnp.bitwise_and(lax.broadcasted_iota(jnp.int32, (tc, bw), 0), V7X_SUBLANES - 1)
    for c in range(N_RG_BLOCKS):
        sl = slice(c * bw, (c + 1) * bw)
        a, b = _rg_gate_block(xc_buf[:, sl], wa_ref[c], wx_ref[c], ba_ref[:, sl],
                              bx_ref[:, sl], lam_ref[:, sl], first_pos)
        for s in (1, 2, 4):
            keep = sub >= s
            a_sh = pltpu.roll(a, s, axis=0)
            b_sh = pltpu.roll(b, s, axis=0)
            b = jnp.where(keep, a * b_sh + b, b)
            a = jnp.where(keep, a * a_sh, a)
        a_buf[:, sl] = a
        b_buf[:, sl] = b

    def group(gi, h_prev):
        r0 = pl.multiple_of(gi * V7X_SUBLANES, V7X_SUBLANES)
        h = b_buf[pl.ds(r0, V7X_SUBLANES), :] + a_buf[pl.ds(r0, V7X_SUBLANES), :] * h_prev
        b_buf[pl.ds(r0, V7X_SUBLANES), :] = h
        return jnp.broadcast_to(h[V7X_SUBLANES - 1:V7X_SUBLANES, :], h.shape)

    h_last = lax.fori_loop(0, tc // V7X_SUBLANES, group, h_carry[...])
    h_carry[...] = h_last
    hlast_ref[...] = h_last[0:1, :]
    hg_ref[...] = (b_buf[...] * gy_ref[...]).astype(hg_ref.dtype)


def _lru_prompt(xr, gy, conv_w, conv_b, wa, wx, ba, bx, lam, *, tc):
    t_p, d = xr.shape
    bw = d // N_RG_BLOCKS
    row = pl.BlockSpec((1, d), lambda t: (0, 0))
    wspec = pl.BlockSpec((N_RG_BLOCKS, bw, bw), lambda t: (0, 0, 0))
    chunk = pl.BlockSpec((tc, d), lambda t: (t, 0))
    pipelined = (2 * _nbytes((tc, d), F32) + _nbytes((tc, d), BF16)
                 + 2 * _nbytes((N_RG_BLOCKS, bw, bw), BF16))
    scratch = [pltpu.VMEM((tc + V7X_SUBLANES, d), F32), pltpu.VMEM((tc, d), F32),
               pltpu.VMEM((tc, d), F32), pltpu.VMEM((tc, d), F32),
               pltpu.VMEM((V7X_SUBLANES, d), F32)]
    resident = 5 * _nbytes((tc, d), F32)
    return pl.pallas_call(
        functools.partial(_lru_prompt_kernel, tc=tc),
        grid=(t_p // tc,),
        in_specs=[chunk, chunk, pl.BlockSpec((CONV_W, d), lambda t: (0, 0)), row,
                  wspec, wspec, row, row, row],
        out_specs=[chunk, row],
        out_shape=[jax.ShapeDtypeStruct((t_p, d), BF16),
                   jax.ShapeDtypeStruct((1, d), F32)],
        scratch_shapes=scratch,
        compiler_params=_params(1, pipelined, resident),
        name="lru_prompt",
    )(xr, gy, conv_w, conv_b, wa, wx, ba, bx, lam)


def _lru_sample_kernel(xr_ref, gy_ref, s0_ref, s1_ref, s2_ref, h0_ref, cw_ref, cb_ref,
                       wa_ref, wx_ref, ba_ref, bx_ref, lam_ref, hg_ref, h_ref):
    xc = cb_ref[...]
    for j, s_ref in enumerate((s0_ref, s1_ref, s2_ref)):
        xc = xc + cw_ref[j:j + 1, :] * s_ref[...]
    xc = xc + cw_ref[CONV_W - 1:CONV_W, :] * xr_ref[...]
    a, u = _rg_gate_block(xc, wa_ref[0], wx_ref[0], ba_ref[...], bx_ref[...],
                          lam_ref[...], None)
    h = u + a * h0_ref[...]
    h_ref[...] = h
    hg_ref[...] = (h * gy_ref[...]).astype(hg_ref.dtype)


def _lru_sample(xr, gy, state_conv, state_h, conv_w, conv_b, wa, wx, ba, bx, lam):
    t_s, d = xr.shape
    bw = d // N_RG_BLOCKS
    assert CONV_W == 4
    tile = pl.BlockSpec((t_s, bw), lambda c: (0, c))
    row = pl.BlockSpec((1, bw), lambda c: (0, c))
    wspec = pl.BlockSpec((1, bw, bw), lambda c: (c, 0, 0))
    state = [pl.BlockSpec((t_s, bw), lambda c, j=j: (0, j * N_RG_BLOCKS + c))
             for j in range(CONV_W - 1)]
    pipelined = 9 * _nbytes((t_s, bw), F32) + 2 * _nbytes((bw, bw), BF16)
    return pl.pallas_call(
        _lru_sample_kernel,
        grid=(N_RG_BLOCKS,),
        in_specs=[tile, tile] + state + [tile, pl.BlockSpec((CONV_W, bw), lambda c: (0, c)),
                                         row, wspec, wspec, row, row, row],
        out_specs=[tile, tile],
        out_shape=[jax.ShapeDtypeStruct((t_s, d), BF16),
                   jax.ShapeDtypeStruct((t_s, d), F32)],
        compiler_params=_params(1, pipelined, 8 * _nbytes((t_s, bw), F32)),
        name="lru_sample",
    )(xr, gy, state_conv, state_conv, state_conv, state_h, conv_w, conv_b, wa, wx,
      ba, bx, lam)


def _alibi_slope(head, n_heads):
    return 2.0 ** (-8.0 * (head + 1) / n_heads)


def _softmax_sink(s, sink):
    m = jnp.maximum(jnp.max(s, axis=-1, keepdims=True), sink)
    p = jnp.exp(s - m)
    return p, jnp.sum(p, axis=-1, keepdims=True) + jnp.exp(sink - m)


def _attn_prompt_kernel(sinks_ref, q_ref, kp_ref, kc_ref, vp_ref, vc_ref, o_ref, *,
                        n_heads):
    n = pl.program_id(0)
    blk = q_ref.shape[0]
    group = n_heads // N_KV_HEADS
    qi = lax.broadcasted_iota(jnp.int32, (blk, 2 * blk), 0)
    kj = lax.broadcasted_iota(jnp.int32, (blk, 2 * blk), 1)
    dist = qi + blk - kj
    exists = (kj + n * blk) >= blk
    valid = (dist >= 0) & (dist < WINDOW) & exists
    dist_f = dist.astype(F32)
    scale = HEAD_DIM ** -0.5
    for kv in range(N_KV_HEADS):
        cols = slice(kv * HEAD_DIM, (kv + 1) * HEAD_DIM)
        keys = jnp.concatenate([kp_ref[:, cols], kc_ref[:, cols]], axis=0)
        vals = jnp.concatenate([vp_ref[:, cols], vc_ref[:, cols]], axis=0)
        for g in range(group):
            head = kv * group + g
            hc = slice(head * HEAD_DIM, (head + 1) * HEAD_DIM)
            s = lax.dot_general(q_ref[:, hc], keys, (((1,), (1,)), ((), ())),
                                preferred_element_type=F32)
            s = s * scale - _alibi_slope(head, n_heads) * dist_f
            s = jnp.where(valid, s, -jnp.inf)
            sink = jnp.full((blk, 1), sinks_ref[head], F32)
            p, denom = _softmax_sink(s, sink)
            o = jnp.dot(p.astype(BF16), vals, preferred_element_type=F32)
            o_ref[:, hc] = (o / denom).astype(o_ref.dtype)


def _attn_prompt(q, k, v, sinks):
    t_p, dq = q.shape
    dk = k.shape[1]
    blk = WINDOW
    cur = lambda n: (n, 0)
    prev = lambda n: (jnp.maximum(n - 1, 0), 0)
    pipelined = 2 * _nbytes((blk, dq), BF16) + 4 * _nbytes((blk, dk), BF16)
    return pl.pallas_call(
        functools.partial(_attn_prompt_kernel, n_heads=dq // HEAD_DIM),
        grid=(t_p // blk,),
        in_specs=[pl.BlockSpec(memory_space=pltpu.SMEM),
                  pl.BlockSpec((blk, dq), cur),
                  pl.BlockSpec((blk, dk), prev), pl.BlockSpec((blk, dk), cur),
                  pl.BlockSpec((blk, dk), prev), pl.BlockSpec((blk, dk), cur)],
        out_specs=pl.BlockSpec((blk, dq), cur),
        out_shape=jax.ShapeDtypeStruct((t_p, dq), BF16),
        compiler_params=_params(1, pipelined, 16 * _nbytes((blk, 2 * blk), F32)),
        name="attn_prompt",
    )(sinks, q, k, k, v, v)


def _attn_sample_kernel(sink_ref, q_ref, kn_ref, vn_ref, ck_ref, cv_ref,
                        o_ref, ok_ref, ov_ref, *, n_heads, bb):
    group = n_heads // N_KV_HEADS
    dk = N_KV_HEADS * HEAD_DIM
    row_kv = _shift_div(lax.broadcasted_iota(jnp.int32, (n_heads, dk), 0), group)
    col_kv = _shift_div(lax.broadcasted_iota(jnp.int32, (n_heads, dk), 1), HEAD_DIM)
    diag = row_kv == col_kv
    slot = lax.broadcasted_iota(jnp.int32, (n_heads, WINDOW), 1)
    dist = jnp.where(slot == 0, 0, WINDOW - slot).astype(F32)
    head_col = lax.broadcasted_iota(jnp.int32, (n_heads, 1), 0)
    slope = jnp.exp2(-8.0 * (head_col + 1).astype(F32) / n_heads)
    head_kv = _shift_div(lax.broadcasted_iota(jnp.int32, (n_heads, HEAD_DIM), 0), group)
    sink = sink_ref[...]
    first_row = lax.broadcasted_iota(jnp.int32, (WINDOW, dk), 0) == 0
    scale = HEAD_DIM ** -0.5
    tail = WINDOW * N_KV_HEADS

    def gather(c_ref, b, new_row):
        per_kv = [c_ref[b, pl.ds(kv, WINDOW, stride=N_KV_HEADS), :]
                  for kv in range(N_KV_HEADS)]
        wide = jnp.concatenate(per_kv, axis=1)
        return jnp.where(first_row, new_row, wide).astype(BF16)

    for b in range(bb):
        keys = gather(ck_ref, b, kn_ref[b:b + 1, :])
        vals = gather(cv_ref, b, vn_ref[b:b + 1, :])
        q = q_ref[b]
        q_wide = jnp.where(diag, jnp.concatenate([q] * N_KV_HEADS, axis=1),
                           jnp.zeros((), BF16))
        s = lax.dot_general(q_wide, keys, (((1,), (1,)), ((), ())),
                            preferred_element_type=F32)
        s = s * scale - slope * dist
        p, denom = _softmax_sink(s, sink)
        o_wide = jnp.dot(p.astype(BF16), vals, preferred_element_type=F32)
        o = jnp.zeros((n_heads, HEAD_DIM), F32)
        for kv in range(N_KV_HEADS):
            o = o + jnp.where(head_kv == kv,
                              o_wide[:, kv * HEAD_DIM:(kv + 1) * HEAD_DIM], 0.0)
        o_ref[b] = (o / denom).astype(o_ref.dtype)
        for c_ref, n_ref, out_ref in ((ck_ref, kn_ref, ok_ref), (cv_ref, vn_ref, ov_ref)):
            out_ref[b, 0:tail - N_KV_HEADS, :] = c_ref[b, N_KV_HEADS:tail, :]
            for kv in range(N_KV_HEADS):
                r = tail - N_KV_HEADS + kv
                out_ref[b, r:r + 1, :] = n_ref[b:b + 1, kv * HEAD_DIM:(kv + 1) * HEAD_DIM]


def _attn_sample(q, k_new, v_new, cache_k, cache_v, sink_col, *, bb):
    t_s, n_heads, _ = q.shape
    dk = N_KV_HEADS * HEAD_DIM
    rows = WINDOW * N_KV_HEADS
    cache = pl.BlockSpec((bb, rows, HEAD_DIM), lambda i: (i, 0, 0))
    new = pl.BlockSpec((bb, dk), lambda i: (i, 0))
    qspec = pl.BlockSpec((bb, n_heads, HEAD_DIM), lambda i: (i, 0, 0))
    pipelined = (4 * _nbytes((bb, rows, HEAD_DIM), F32) + 2 * _nbytes((bb, dk), F32)
                 + 2 * _nbytes((bb, n_heads, HEAD_DIM), BF16))
    return pl.pallas_call(
        functools.partial(_attn_sample_kernel, n_heads=n_heads, bb=bb),
        grid=(t_s // bb,),
        in_specs=[pl.BlockSpec((n_heads, 1), lambda i: (0, 0)), qspec, new, new,
                  cache, cache],
        out_specs=[qspec, cache, cache],
        out_shape=[jax.ShapeDtypeStruct((t_s, n_heads, HEAD_DIM), BF16),
                   jax.ShapeDtypeStruct((t_s, rows, HEAD_DIM), F32),
                   jax.ShapeDtypeStruct((t_s, rows, HEAD_DIM), F32)],
        compiler_params=_params(1, pipelined, 8 * _nbytes((WINDOW, dk), F32)),
        name="attn_sample",
    )(sink_col, q, k_new, v_new, cache_k, cache_v)


def _swiglu_ffn(x, norm, w_gate, w_up, w_down, *, tm, tn):
    h = tuple(_rmsnorm(xi, norm) for xi in x)
    d_ff = w_gate.shape[1]
    (a,) = _matmul("ffn_gate_up", [h], [(w_gate, 0), (w_up, 0)], [], [BF16], _ep_swiglu,
                   n_cols=d_ff, tm=tm, tn=tn)
    return _ffn_down(a, w_down, x, tm=tm, tn=tn)


def kernel(x_prompt, x_sample, state_conv, state_h, cache_k, cache_v, norm_ffn1, w_ffn1_gate, w_ffn1_up, w_ffn1_down, norm_mix, w_in, conv_w, conv_b, rg_w_a, rg_b_a, rg_w_x, rg_b_x, rg_lambda, q_norm, k_norm, sinks, w_lru_proj, w_attn_proj, w_out, norm_ffn2, w_ffn2_gate, w_ffn2_up, w_ffn2_down):
    b_p, t_p, d = x_prompt.shape
    t_s = x_sample.shape[0]
    depth = w_in.shape[0]
    assert b_p == 1 and x_sample.shape[1] == 1 and depth == 1
    n_heads = d // HEAD_DIM
    dk = N_KV_HEADS * HEAD_DIM
    tm, tn = 1024, V7X_MXU_DIM
    mm = functools.partial(_matmul, tm=tm, tn=tn)

    x = (x_prompt.reshape(t_p, d), x_sample.reshape(t_s, d))
    x = _swiglu_ffn(x, norm_ffn1[0], w_ffn1_gate[0], w_ffn1_up[0], w_ffn1_down[0],
                    tm=tm, tn=tn)

    xn = tuple(_rmsnorm(xi, norm_mix[0]) for xi in x)
    w = w_in[0]
    (xr,) = mm("in_xr", [xn], [(w, 0)], [], [F32], _ep_identity, n_cols=d)
    (gy,) = mm("in_gelu", [xn], [(w, d)], [], [F32], _ep_gelu, n_cols=d)
    (q,) = mm("in_q", [xn], [(w, 2 * d)], [("const", q_norm[0].reshape(1, HEAD_DIM))],
              [BF16], _ep_head_rms, n_cols=d)
    k32, kb = mm("in_k", [xn], [(w, 3 * d)], [("const", k_norm[0].reshape(1, HEAD_DIM))],
                 [F32, BF16], _ep_head_rms_twice, n_cols=dk)
    v32, vb = mm("in_v", [xn], [(w, 3 * d + dk)], [], [F32, BF16], _ep_twice, n_cols=dk)
    (sg_rec,) = mm("in_gate_rec", [xn], [(w, 3 * d + 2 * dk)], [], [F32], _ep_sigmoid,
                   n_cols=d)
    (sg_att,) = mm("in_gate_att", [xn], [(w, 4 * d + 2 * dk)], [], [F32], _ep_sigmoid,
                   n_cols=d)

    cw, cb = conv_w[0], conv_b[0].reshape(1, d)
    wa, wx = rg_w_a[0].astype(BF16), rg_w_x[0].astype(BF16)
    ba, bx = rg_b_a[0].reshape(1, d), rg_b_x[0].reshape(1, d)
    lam = rg_lambda[0].reshape(1, d)
    hg_p, h_last = _lru_prompt(xr[0], gy[0], cw, cb, wa, wx, ba, bx, lam, tc=256)
    hg_s, h_s = _lru_sample(xr[1], gy[1], state_conv[0].reshape(t_s, (CONV_W - 1) * d),
                            state_h[0], cw, cb, wa, wx, ba, bx, lam)

    o_p = _attn_prompt(q[0], kb[0], vb[0], sinks[0])
    o_s, new_k, new_v = _attn_sample(
        q[1].reshape(t_s, n_heads, HEAD_DIM), k32[1], v32[1],
        cache_k[0].reshape(t_s, WINDOW * N_KV_HEADS, HEAD_DIM),
        cache_v[0].reshape(t_s, WINDOW * N_KV_HEADS, HEAD_DIM),
        sinks[0].reshape(n_heads, 1), bb=8)
    o = (o_p, o_s.reshape(t_s, d))

    (y_rec,) = mm("lru_proj", [(hg_p, hg_s)], [(w_lru_proj[0], 0)], [("pair",) + sg_rec],
                  [F32], _ep_gate_mul, n_cols=d)
    (mixed,) = mm("attn_proj", [o], [(w_attn_proj[0], 0)],
                  [("pair",) + sg_att, ("pair",) + y_rec], [BF16], _ep_gate_mul_add,
                  n_cols=d)
    (x,) = mm("out_proj", [mixed], [(w_out[0], 0)], [("pair",) + x], [F32], _ep_residual,
              n_cols=d)

    y = _swiglu_ffn(x, norm_ffn2[0], w_ffn2_gate[0], w_ffn2_up[0], w_ffn2_down[0],
                    tm=tm, tn=tn)

    kv_shape = (1, -1, WINDOW, N_KV_HEADS, HEAD_DIM)
    n_conv = CONV_W - 1
    sample_conv = jnp.concatenate([state_conv[0][:, 1:], xr[1][:, None, :]], axis=1)
    return (y[0].reshape(1, t_p, d), y[1].reshape(t_s, 1, d),
            xr[0][t_p - n_conv:].reshape(1, 1, n_conv, d), h_last.reshape(1, 1, d),
            k32[0][t_p - WINDOW:].reshape(kv_shape), v32[0][t_p - WINDOW:].reshape(kv_shape),
            sample_conv[None], h_s[None],
            new_k.reshape(kv_shape), new_v.reshape(kv_shape))
```

```python
import functools
import math

import jax
import jax.numpy as jnp
from jax import lax
from jax.experimental import pallas as pl
from jax.experimental.pallas import tpu as pltpu

HEAD_DIM = 128
N_KV_HEADS = 8
WINDOW = 128
N_RG_BLOCKS = 16
CONV_W = 4
RG_C = 8.0
NORM_EPS = 1e-6
MACARON_WEIGHT = 0.5

V7X_VMEM_BYTES = 64 * 1024 * 1024
V7X_LANES = 128
V7X_SUBLANES = 8
V7X_MXU_DIM = 256
V7X_VMEM_RESERVE_BYTES = 4 * 1024 * 1024
V7X_VMEM_MAX_REQUEST_BYTES = V7X_VMEM_BYTES - 1024 * 1024

F32 = jnp.float32
BF16 = jnp.bfloat16


def _nbytes(shape, dtype):
    return math.prod(shape) * jnp.dtype(dtype).itemsize


def _vmem_limit(pipelined_bytes, resident_bytes):
    want = 2 * pipelined_bytes + resident_bytes + V7X_VMEM_RESERVE_BYTES
    return int(min(want, V7X_VMEM_MAX_REQUEST_BYTES))


def _params(n_grid, pipelined_bytes, resident_bytes):
    return pltpu.CompilerParams(
        dimension_semantics=("arbitrary",) * n_grid,
        vmem_limit_bytes=_vmem_limit(pipelined_bytes, resident_bytes))


def _shift_div(x, n):
    assert n & (n - 1) == 0
    return lax.shift_right_logical(x, jnp.int32(n.bit_length() - 1))


def _rms_kernel(x_ref, g_ref, o_ref):
    x = x_ref[...]
    ms = jnp.mean(x * x, axis=-1, keepdims=True)
    o_ref[...] = (x * lax.rsqrt(ms + NORM_EPS) * g_ref[...]).astype(o_ref.dtype)


def _rmsnorm(x, g):
    rows, d = x.shape
    rb = min(rows, 256)
    assert rows % rb == 0
    blk = _nbytes((rb, d), F32) + _nbytes((rb, d), BF16)
    return pl.pallas_call(
        _rms_kernel,
        grid=(rows // rb,),
        in_specs=[pl.BlockSpec((rb, d), lambda i: (i, 0)),
                  pl.BlockSpec((1, d), lambda i: (0, 0))],
        out_specs=pl.BlockSpec((rb, d), lambda i: (i, 0)),
        out_shape=jax.ShapeDtypeStruct((rows, d), BF16),
        compiler_params=_params(1, blk, 3 * _nbytes((rb, d), F32)),
        name="rmsnorm",
    )(x, g.reshape(1, d))


def _tile_maps(n_i, n_j):
    def prompt(i, j, *_):
        return (i, j)

    def sample(i, j, *_):
        return (0, jnp.where(i == n_i - 1, j, 0))

    return prompt, sample


def _mm_kernel(*refs, n_x, n_w, extra_kinds, n_out, epilogue, n_i):
    pos = 0
    xs = [(refs[pos + 2 * t], refs[pos + 2 * t + 1]) for t in range(n_x)]
    pos += 2 * n_x
    ws = list(refs[pos:pos + n_w])
    pos += n_w
    extras = []
    for kind in extra_kinds:
        if kind == "pair":
            extras.append((refs[pos], refs[pos + 1]))
            pos += 2
        else:
            extras.append((refs[pos], refs[pos]))
            pos += 1
    outs = [(refs[pos + 2 * t], refs[pos + 2 * t + 1]) for t in range(n_out)]
    i = pl.program_id(0)
    wb = [w[...].astype(BF16) for w in ws]

    def compute(which):
        accs = []
        for t in range(n_w):
            x = xs[t if n_x > 1 else 0][which][...]
            accs.append(jnp.dot(x, wb[t], preferred_element_type=F32))
        res = epilogue(accs, [e[which][...] for e in extras])
        for o, r in zip(outs, res):
            o[which][...] = r.astype(o[which].dtype)

    compute(0)

    @pl.when(i == n_i - 1)
    def _():
        compute(1)


def _matmul(name, xs, ws, extras, out_dtypes, epilogue, *, n_cols, tm, tn):
    t_p, k = xs[0][0].shape
    t_s = xs[0][1].shape[0]
    assert t_p % tm == 0 and n_cols % tn == 0
    n_i = t_p // tm
    n_j = n_cols // tn
    map_p, map_s = _tile_maps(n_i, n_j)

    in_specs, args = [], []
    pipelined = 0
    for w, col0 in ws:
        assert col0 % tn == 0 and w.shape[0] == k
        in_specs.append(pl.BlockSpec((k, tn), lambda i, j, c=col0 // tn: (0, j + c)))
        args.append(w)
        pipelined += _nbytes((k, tn), w.dtype)
    kinds = []
    for e in extras:
        kinds.append(e[0])
        if e[0] == "pair":
            in_specs += [pl.BlockSpec((tm, tn), map_p), pl.BlockSpec((t_s, tn), map_s)]
            args += [e[1], e[2]]
            pipelined += _nbytes((tm + t_s, tn), e[1].dtype)
        elif e[0] == "row":
            in_specs.append(pl.BlockSpec((1, tn), lambda i, j: (0, j)))
            args.append(e[1])
        else:
            assert e[0] == "const" and e[1].ndim == 2
            in_specs.append(pl.BlockSpec(e[1].shape, lambda i, j: (0, 0)))
            args.append(e[1])
    out_specs, out_shape = [], []
    for dt in out_dtypes:
        out_specs += [pl.BlockSpec((tm, tn), map_p), pl.BlockSpec((t_s, tn), map_s)]
        out_shape += [jax.ShapeDtypeStruct((t_p, n_cols), dt),
                      jax.ShapeDtypeStruct((t_s, n_cols), dt)]
        pipelined += _nbytes((tm + t_s, tn), dt)
    resident = len(ws) * (_nbytes((k, tn), BF16) + 4 * _nbytes((tm, tn), F32))
    pipelined += len(xs) * _nbytes((t_s, k), BF16)
    x_bytes = len(xs) * _nbytes((tm, k), BF16)
    double_x = (2 * (pipelined + x_bytes) + resident + V7X_VMEM_RESERVE_BYTES
                <= V7X_VMEM_MAX_REQUEST_BYTES)
    if double_x:
        x_spec = pl.BlockSpec((tm, k), lambda i, j: (i, 0))
        pipelined += x_bytes
    else:
        x_spec = pl.BlockSpec((tm, k), lambda i, j: (i, 0), pipeline_mode=pl.Buffered(1))
        resident += x_bytes
    x_specs, x_args = [], []
    for x_p, x_s in xs:
        assert x_p.dtype == BF16 and x_s.dtype == BF16
        x_specs += [x_spec, pl.BlockSpec((t_s, k), lambda i, j: (0, 0))]
        x_args += [x_p, x_s]
    in_specs = x_specs + in_specs
    args = x_args + args
    res = pl.pallas_call(
        functools.partial(_mm_kernel, n_x=len(xs), n_w=len(ws), extra_kinds=tuple(kinds),
                          n_out=len(out_dtypes), epilogue=epilogue, n_i=n_i),
        grid=(n_i, n_j),
        in_specs=in_specs,
        out_specs=out_specs,
        out_shape=out_shape,
        compiler_params=_params(2, pipelined, resident),
        name=name,
    )(*args)
    return [(res[2 * t], res[2 * t + 1]) for t in range(len(out_dtypes))]


def _sigmoid(x):
    return 1.0 / (1.0 + jnp.exp(-x))


def _ep_identity(accs, extras):
    return [accs[0]]


def _ep_twice(accs, extras):
    return [accs[0], accs[0]]


def _ep_swiglu(accs, extras):
    g, u = accs
    return [g * _sigmoid(g) * u]


def _ep_gelu(accs, extras):
    x = accs[0]
    c = math.sqrt(2.0 / math.pi)
    return [0.5 * x * (1.0 + jnp.tanh(c * (x + 0.044715 * (x * x * x))))]


def _ep_sigmoid(accs, extras):
    return [_sigmoid(accs[0])]


def _head_rms(acc, gain):
    parts = []
    for h in range(acc.shape[1] // HEAD_DIM):
        y = acc[:, h * HEAD_DIM:(h + 1) * HEAD_DIM]
        ms = jnp.mean(y * y, axis=-1, keepdims=True)
        parts.append(y * lax.rsqrt(ms + NORM_EPS) * gain)
    return jnp.concatenate(parts, axis=1)


def _ep_head_rms(accs, extras):
    return [_head_rms(accs[0], extras[0])]


def _ep_head_rms_twice(accs, extras):
    y = _head_rms(accs[0], extras[0])
    return [y, y]


def _ep_gate_mul(accs, extras):
    return [extras[0] * accs[0]]


def _ep_gate_mul_add(accs, extras):
    return [extras[1] + extras[0] * accs[0]]


def _ep_residual(accs, extras):
    return [extras[0] + accs[0]]


def _down_kernel(a_p, a_s, w_ref, r_p, r_s, o_p, o_s, *, n_i, k_half):
    i = pl.program_id(0)
    kk = pl.program_id(2)
    w = w_ref[...].astype(BF16)
    k0 = pl.multiple_of(kk * k_half, V7X_LANES)

    def compute(a_ref, r_ref, o_ref):
        d = MACARON_WEIGHT * jnp.dot(a_ref[:, pl.ds(k0, k_half)], w,
                                     preferred_element_type=F32)
        @pl.when(kk == 0)
        def _():
            o_ref[...] = r_ref[...] + d

        @pl.when(kk != 0)
        def _():
            o_ref[...] += d

    compute(a_p, r_p, o_p)

    @pl.when(i == n_i - 1)
    def _():
        compute(a_s, r_s, o_s)


def _ffn_down(a, w, res, *, tm, tn):
    a_p, a_s = a
    r_p, r_s = res
    t_p, k = a_p.shape
    t_s = a_s.shape[0]
    n_cols = w.shape[1]
    assert k % 2 == 0 and (k // 2) % V7X_LANES == 0
    k_half = k // 2
    n_i = t_p // tm
    n_j = n_cols // tn
    map_p, map_s = _tile_maps(n_i, n_j)
    tile_p = pl.BlockSpec((tm, tn), map_p)
    tile_s = pl.BlockSpec((t_s, tn), map_s)
    pipelined = (_nbytes((k_half, tn), F32) + 2 * _nbytes((tm + t_s, tn), F32)
                 + _nbytes((t_s, k), BF16))
    resident = (_nbytes((tm, k), BF16) + _nbytes((k_half, tn), BF16)
                + 2 * _nbytes((tm, tn), F32))
    o_p, o_s = pl.pallas_call(
        functools.partial(_down_kernel, n_i=n_i, k_half=k_half),
        grid=(n_i, n_j, 2),
        in_specs=[
            pl.BlockSpec((tm, k), lambda i, j, kk: (i, 0), pipeline_mode=pl.Buffered(1)),
            pl.BlockSpec((t_s, k), lambda i, j, kk: (0, 0)),
            pl.BlockSpec((k_half, tn), lambda i, j, kk: (kk, j)),
            tile_p, tile_s],
        out_specs=[tile_p, tile_s],
        out_shape=[jax.ShapeDtypeStruct((t_p, n_cols), F32),
                   jax.ShapeDtypeStruct((t_s, n_cols), F32)],
        compiler_params=_params(3, pipelined, resident),
        name="ffn_down",
    )(a_p, a_s, w, r_p, r_s)
    return o_p, o_s


def _log_sigmoid(x):
    return jnp.minimum(x, 0.0) - jnp.log1p(jnp.exp(-jnp.abs(x)))


def _rg_gate_block(xc, wa, wx, ba, bx, lam, first_pos):
    xb = xc.astype(BF16)
    r = _sigmoid(jnp.dot(xb, wa, preferred_element_type=F32) + ba)
    g = _sigmoid(jnp.dot(xb, wx, preferred_element_type=F32) + bx)
    a = jnp.exp(r * (RG_C * _log_sigmoid(lam)))
    v = 1.0 - a * a
    mult = jnp.where(v > 0.0, v * lax.rsqrt(v), 0.0)
    if first_pos is not None:
        mult = jnp.where(first_pos, 1.0, mult)
    return a, mult * g * xc


def _lru_prompt_kernel(xr_ref, gy_ref, cw_ref, cb_ref, wa_ref, wx_ref, ba_ref, bx_ref,
                       lam_ref, hg_ref, hlast_ref, x_tail, a_buf, b_buf, h_carry, *, tc):
    t = pl.program_id(0)
    d = xr_ref.shape[1]
    bw = d // N_RG_BLOCKS
    pad = V7X_SUBLANES

    @pl.when(t == 0)
    def _():
        x_tail[...] = jnp.zeros_like(x_tail)
        h_carry[...] = jnp.zeros_like(h_carry)

    row = lax.broadcasted_iota(jnp.int32, (tc, 1), 0)
    first_pos = (row + t * tc) == 0
    sub = lax.broadcasted_iota(jnp.int32, (tc // V7X_SUBLANES, V7X_SUBLANES, bw), 1)
    for c in range(N_RG_BLOCKS):
        sl = slice(c * bw, (c + 1) * bw)
        x = xr_ref[:, sl]
        x_ext = jnp.concatenate([x_tail[:, sl], x], axis=0)
        x_tail[:, sl] = x[tc - pad:, :]
        shifted = [x_ext]
        for _ in range(CONV_W - 1):
            shifted.append(pltpu.roll(shifted[-1], 1, axis=0))
        xc = cb_ref[:, sl]
        for j in range(CONV_W):
            back = CONV_W - 1 - j
            xs = x if back == 0 else shifted[back][pad:, :]
            xc = xc + cw_ref[j:j + 1, sl] * xs
        a, b = _rg_gate_block(xc, wa_ref[c], wx_ref[c], ba_ref[:, sl], bx_ref[:, sl],
                              lam_ref[:, sl], first_pos)
        a = a.reshape(tc // V7X_SUBLANES, V7X_SUBLANES, bw)
        b = b.reshape(tc // V7X_SUBLANES, V7X_SUBLANES, bw)
        for s in (1, 2, 4):
            keep = sub >= s
            a_sh = pltpu.roll(a, s, axis=1)
            b_sh = pltpu.roll(b, s, axis=1)
            b = jnp.where(keep, a * b_sh + b, b)
            a = jnp.where(keep, a * a_sh, a)
        a_buf[:, sl] = a.reshape(tc, bw)
        b_buf[:, sl] = b.reshape(tc, bw)

    def group(gi, h_prev):
        r0 = pl.multiple_of(gi * V7X_SUBLANES, V7X_SUBLANES)
        h = b_buf[pl.ds(r0, V7X_SUBLANES), :] + a_buf[pl.ds(r0, V7X_SUBLANES), :] * h_prev
        b_buf[pl.ds(r0, V7X_SUBLANES), :] = h
        return jnp.broadcast_to(h[V7X_SUBLANES - 1:V7X_SUBLANES, :], h.shape)

    h_last = lax.fori_loop(0, tc // V7X_SUBLANES, group, h_carry[...])
    h_carry[...] = h_last
    hlast_ref[...] = h_last[0:1, :]
    hg_ref[...] = (b_buf[...] * gy_ref[...]).astype(hg_ref.dtype)


def _lru_prompt(xr, gy, conv_w, conv_b, wa, wx, ba, bx, lam, *, tc):
    t_p, d = xr.shape
    bw = d // N_RG_BLOCKS
    row = pl.BlockSpec((1, d), lambda t: (0, 0))
    wspec = pl.BlockSpec((N_RG_BLOCKS, bw, bw), lambda t: (0, 0, 0))
    chunk = pl.BlockSpec((tc, d), lambda t: (t, 0))
    pipelined = (2 * _nbytes((tc, d), F32) + _nbytes((tc, d), BF16)
                 + 2 * _nbytes((N_RG_BLOCKS, bw, bw), BF16))
    scratch = [pltpu.VMEM((V7X_SUBLANES, d), F32), pltpu.VMEM((tc, d), F32),
               pltpu.VMEM((tc, d), F32), pltpu.VMEM((V7X_SUBLANES, d), F32)]
    resident = 4 * _nbytes((tc, d), F32)
    return pl.pallas_call(
        functools.partial(_lru_prompt_kernel, tc=tc),
        grid=(t_p // tc,),
        in_specs=[chunk, chunk, pl.BlockSpec((CONV_W, d), lambda t: (0, 0)), row,
                  wspec, wspec, row, row, row],
        out_specs=[chunk, row],
        out_shape=[jax.ShapeDtypeStruct((t_p, d), BF16),
                   jax.ShapeDtypeStruct((1, d), F32)],
        scratch_shapes=scratch,
        compiler_params=_params(1, pipelined, resident),
        name="lru_prompt",
    )(xr, gy, conv_w, conv_b, wa, wx, ba, bx, lam)


def _lru_sample_kernel(xr_ref, gy_ref, s0_ref, s1_ref, s2_ref, h0_ref, cw_ref, cb_ref,
                       wa_ref, wx_ref, ba_ref, bx_ref, lam_ref, hg_ref, h_ref):
    xc = cb_ref[...]
    for j, s_ref in enumerate((s0_ref, s1_ref, s2_ref)):
        xc = xc + cw_ref[j:j + 1, :] * s_ref[...]
    xc = xc + cw_ref[CONV_W - 1:CONV_W, :] * xr_ref[...]
    a, u = _rg_gate_block(xc, wa_ref[0], wx_ref[0], ba_ref[...], bx_ref[...],
                          lam_ref[...], None)
    h = u + a * h0_ref[...]
    h_ref[...] = h
    hg_ref[...] = (h * gy_ref[...]).astype(hg_ref.dtype)


def _lru_sample(xr, gy, state_conv, state_h, conv_w, conv_b, wa, wx, ba, bx, lam):
    t_s, d = xr.shape
    bw = d // N_RG_BLOCKS
    assert CONV_W == 4
    tile = pl.BlockSpec((t_s, bw), lambda c: (0, c))
    row = pl.BlockSpec((1, bw), lambda c: (0, c))
    wspec = pl.BlockSpec((1, bw, bw), lambda c: (c, 0, 0))
    state = [pl.BlockSpec((t_s, bw), lambda c, j=j: (0, j * N_RG_BLOCKS + c))
             for j in range(CONV_W - 1)]
    pipelined = 9 * _nbytes((t_s, bw), F32) + 2 * _nbytes((bw, bw), BF16)
    return pl.pallas_call(
        _lru_sample_kernel,
        grid=(N_RG_BLOCKS,),
        in_specs=[tile, tile] + state + [tile, pl.BlockSpec((CONV_W, bw), lambda c: (0, c)),
                                         row, wspec, wspec, row, row, row],
        out_specs=[tile, tile],
        out_shape=[jax.ShapeDtypeStruct((t_s, d), BF16),
                   jax.ShapeDtypeStruct((t_s, d), F32)],
        compiler_params=_params(1, pipelined, 8 * _nbytes((t_s, bw), F32)),
        name="lru_sample",
    )(xr, gy, state_conv, state_conv, state_conv, state_h, conv_w, conv_b, wa, wx,
      ba, bx, lam)


def _alibi_slope(head, n_heads):
    return 2.0 ** (-8.0 * (head + 1) / n_heads)


def _softmax_sink(s, sink):
    m = jnp.maximum(jnp.max(s, axis=-1, keepdims=True), sink)
    p = jnp.exp(s - m)
    return p, jnp.sum(p, axis=-1, keepdims=True) + jnp.exp(sink - m)


def _attn_prompt_kernel(sinks_ref, q_ref, kp_ref, kc_ref, vp_ref, vc_ref, o_ref, *,
                        n_heads):
    n = pl.program_id(0)
    blk = q_ref.shape[0]
    group = n_heads // N_KV_HEADS
    qi = lax.broadcasted_iota(jnp.int32, (blk, 2 * blk), 0)
    kj = lax.broadcasted_iota(jnp.int32, (blk, 2 * blk), 1)
    dist = qi + blk - kj
    exists = (kj + n * blk) >= blk
    valid = (dist >= 0) & (dist < WINDOW) & exists
    dist_f = dist.astype(F32)
    scale = HEAD_DIM ** -0.5
    for kv in range(N_KV_HEADS):
        cols = slice(kv * HEAD_DIM, (kv + 1) * HEAD_DIM)
        keys = jnp.concatenate([kp_ref[:, cols], kc_ref[:, cols]], axis=0)
        vals = jnp.concatenate([vp_ref[:, cols], vc_ref[:, cols]], axis=0)
        for g in range(group):
            head = kv * group + g
            hc = slice(head * HEAD_DIM, (head + 1) * HEAD_DIM)
            s = lax.dot_general(q_ref[:, hc], keys, (((1,), (1,)), ((), ())),
                                preferred_element_type=F32)
            s = s * scale - _alibi_slope(head, n_heads) * dist_f
            s = jnp.where(valid, s, -jnp.inf)
            sink = jnp.full((blk, 1), sinks_ref[head], F32)
            p, denom = _softmax_sink(s, sink)
            o = jnp.dot(p.astype(BF16), vals, preferred_element_type=F32)
            o_ref[:, hc] = (o / denom).astype(o_ref.dtype)


def _attn_prompt(q, k, v, sinks):
    t_p, dq = q.shape
    dk = k.shape[1]
    blk = WINDOW
    cur = lambda n: (n, 0)
    prev = lambda n: (jnp.maximum(n - 1, 0), 0)
    pipelined = 2 * _nbytes((blk, dq), BF16) + 4 * _nbytes((blk, dk), BF16)
    return pl.pallas_call(
        functools.partial(_attn_prompt_kernel, n_heads=dq // HEAD_DIM),
        grid=(t_p // blk,),
        in_specs=[pl.BlockSpec(memory_space=pltpu.SMEM),
                  pl.BlockSpec((blk, dq), cur),
                  pl.BlockSpec((blk, dk), prev), pl.BlockSpec((blk, dk), cur),
                  pl.BlockSpec((blk, dk), prev), pl.BlockSpec((blk, dk), cur)],
        out_specs=pl.BlockSpec((blk, dq), cur),
        out_shape=jax.ShapeDtypeStruct((t_p, dq), BF16),
        compiler_params=_params(1, pipelined, 16 * _nbytes((blk, 2 * blk), F32)),
        name="attn_prompt",
    )(sinks, q, k, k, v, v)


def _attn_sample_kernel(sink_ref, q_ref, kn_ref, vn_ref, ck_ref, cv_ref,
                        o_ref, ok_ref, ov_ref, *, n_heads, bb):
    group = n_heads // N_KV_HEADS
    dk = N_KV_HEADS * HEAD_DIM
    row_kv = _shift_div(lax.broadcasted_iota(jnp.int32, (n_heads, dk), 0), group)
    col_kv = _shift_div(lax.broadcasted_iota(jnp.int32, (n_heads, dk), 1), HEAD_DIM)
    diag = row_kv == col_kv
    slot = lax.broadcasted_iota(jnp.int32, (n_heads, WINDOW), 1)
    dist = jnp.where(slot == 0, 0, WINDOW - slot).astype(F32)
    head_col = lax.broadcasted_iota(jnp.int32, (n_heads, 1), 0)
    slope = jnp.exp2(-8.0 * (head_col + 1).astype(F32) / n_heads)
    head_kv = _shift_div(lax.broadcasted_iota(jnp.int32, (n_heads, HEAD_DIM), 0), group)
    sink = sink_ref[...]
    first_row = lax.broadcasted_iota(jnp.int32, (WINDOW, dk), 0) == 0
    scale = HEAD_DIM ** -0.5
    tail = WINDOW * N_KV_HEADS

    def gather(c_ref, b, new_row):
        per_kv = [c_ref[b, pl.ds(kv, WINDOW, stride=N_KV_HEADS), :]
                  for kv in range(N_KV_HEADS)]
        wide = jnp.concatenate(per_kv, axis=1)
        return jnp.where(first_row, new_row, wide).astype(BF16)

    for b in range(bb):
        keys = gather(ck_ref, b, kn_ref[b:b + 1, :])
        vals = gather(cv_ref, b, vn_ref[b:b + 1, :])
        q = q_ref[b]
        q_wide = jnp.where(diag, jnp.concatenate([q] * N_KV_HEADS, axis=1),
                           jnp.zeros((), BF16))
        s = lax.dot_general(q_wide, keys, (((1,), (1,)), ((), ())),
                            preferred_element_type=F32)
        s = s * scale - slope * dist
        p, denom = _softmax_sink(s, sink)
        o_wide = jnp.dot(p.astype(BF16), vals, preferred_element_type=F32)
        o = jnp.zeros((n_heads, HEAD_DIM), F32)
        for kv in range(N_KV_HEADS):
            o = o + jnp.where(head_kv == kv,
                              o_wide[:, kv * HEAD_DIM:(kv + 1) * HEAD_DIM], 0.0)
        o_ref[b] = (o / denom).astype(o_ref.dtype)
        for c_ref, n_ref, out_ref in ((ck_ref, kn_ref, ok_ref), (cv_ref, vn_ref, ov_ref)):
            out_ref[b, 0:tail - N_KV_HEADS, :] = c_ref[b, N_KV_HEADS:tail, :]
            for kv in range(N_KV_HEADS):
                r = tail - N_KV_HEADS + kv
                out_ref[b, r:r + 1, :] = n_ref[b:b + 1, kv * HEAD_DIM:(kv + 1) * HEAD_DIM]


def _attn_sample(q, k_new, v_new, cache_k, cache_v, sink_col, *, bb):
    t_s, n_heads, _ = q.shape
    dk = N_KV_HEADS * HEAD_DIM
    rows = WINDOW * N_KV_HEADS
    cache = pl.BlockSpec((bb, rows, HEAD_DIM), lambda i: (i, 0, 0))
    new = pl.BlockSpec((bb, dk), lambda i: (i, 0))
    qspec = pl.BlockSpec((bb, n_heads, HEAD_DIM), lambda i: (i, 0, 0))
    pipelined = (4 * _nbytes((bb, rows, HEAD_DIM), F32) + 2 * _nbytes((bb, dk), F32)
                 + 2 * _nbytes((bb, n_heads, HEAD_DIM), BF16))
    return pl.pallas_call(
        functools.partial(_attn_sample_kernel, n_heads=n_heads, bb=bb),
        grid=(t_s // bb,),
        in_specs=[pl.BlockSpec((n_heads, 1), lambda i: (0, 0)), qspec, new, new,
                  cache, cache],
        out_specs=[qspec, cache, cache],
        out_shape=[jax.ShapeDtypeStruct((t_s, n_heads, HEAD_DIM), BF16),
                   jax.ShapeDtypeStruct((t_s, rows, HEAD_DIM), F32),
                   jax.ShapeDtypeStruct((t_s, rows, HEAD_DIM), F32)],
        compiler_params=_params(1, pipelined, 8 * _nbytes((WINDOW, dk), F32)),
        name="attn_sample",
    )(sink_col, q, k_new, v_new, cache_k, cache_v)


def _swiglu_ffn(x, norm, w_gate, w_up, w_down, *, tm, tm_down, tn):
    h = tuple(_rmsnorm(xi, norm) for xi in x)
    d_ff = w_gate.shape[1]
    (a,) = _matmul("ffn_gate_up", [h], [(w_gate, 0), (w_up, 0)], [], [BF16], _ep_swiglu,
                   n_cols=d_ff, tm=tm, tn=tn)
    return _ffn_down(a, w_down, x, tm=tm_down, tn=tn)


def kernel(x_prompt, x_sample, state_conv, state_h, cache_k, cache_v, norm_ffn1, w_ffn1_gate, w_ffn1_up, w_ffn1_down, norm_mix, w_in, conv_w, conv_b, rg_w_a, rg_b_a, rg_w_x, rg_b_x, rg_lambda, q_norm, k_norm, sinks, w_lru_proj, w_attn_proj, w_out, norm_ffn2, w_ffn2_gate, w_ffn2_up, w_ffn2_down):
    b_p, t_p, d = x_prompt.shape
    t_s = x_sample.shape[0]
    depth = w_in.shape[0]
    assert b_p == 1 and x_sample.shape[1] == 1 and depth == 1
    n_heads = d // HEAD_DIM
    dk = N_KV_HEADS * HEAD_DIM
    tm, tm_down, tn = 2048, 1024, V7X_MXU_DIM
    mm = functools.partial(_matmul, tm=tm, tn=tn)
    ffn = functools.partial(_swiglu_ffn, tm=tm, tm_down=tm_down, tn=tn)

    x = (x_prompt.reshape(t_p, d), x_sample.reshape(t_s, d))
    x = ffn(x, norm_ffn1[0], w_ffn1_gate[0], w_ffn1_up[0], w_ffn1_down[0])

    xn = tuple(_rmsnorm(xi, norm_mix[0]) for xi in x)
    w = w_in[0]
    (xr,) = mm("in_xr", [xn], [(w, 0)], [], [F32], _ep_identity, n_cols=d)
    (gy,) = mm("in_gelu", [xn], [(w, d)], [], [F32], _ep_gelu, n_cols=d)
    (q,) = mm("in_q", [xn], [(w, 2 * d)], [("const", q_norm[0].reshape(1, HEAD_DIM))],
              [BF16], _ep_head_rms, n_cols=d)
    k32, kb = mm("in_k", [xn], [(w, 3 * d)], [("const", k_norm[0].reshape(1, HEAD_DIM))],
                 [F32, BF16], _ep_head_rms_twice, n_cols=dk)
    v32, vb = mm("in_v", [xn], [(w, 3 * d + dk)], [], [F32, BF16], _ep_twice, n_cols=dk)
    (sg_rec,) = mm("in_gate_rec", [xn], [(w, 3 * d + 2 * dk)], [], [F32], _ep_sigmoid,
                   n_cols=d)
    (sg_att,) = mm("in_gate_att", [xn], [(w, 4 * d + 2 * dk)], [], [F32], _ep_sigmoid,
                   n_cols=d)

    cw, cb = conv_w[0], conv_b[0].reshape(1, d)
    wa, wx = rg_w_a[0].astype(BF16), rg_w_x[0].astype(BF16)
    ba, bx = rg_b_a[0].reshape(1, d), rg_b_x[0].reshape(1, d)
    lam = rg_lambda[0].reshape(1, d)
    hg_p, h_last = _lru_prompt(xr[0], gy[0], cw, cb, wa, wx, ba, bx, lam, tc=256)
    hg_s, h_s = _lru_sample(xr[1], gy[1], state_conv[0].reshape(t_s, (CONV_W - 1) * d),
                            state_h[0], cw, cb, wa, wx, ba, bx, lam)

    o_p = _attn_prompt(q[0], kb[0], vb[0], sinks[0])
    o_s, new_k, new_v = _attn_sample(
        q[1].reshape(t_s, n_heads, HEAD_DIM), k32[1], v32[1],
        cache_k[0].reshape(t_s, WINDOW * N_KV_HEADS, HEAD_DIM),
        cache_v[0].reshape(t_s, WINDOW * N_KV_HEADS, HEAD_DIM),
        sinks[0].reshape(n_heads, 1), bb=8)
    o = (o_p, o_s.reshape(t_s, d))

    (y_rec,) = mm("lru_proj", [(hg_p, hg_s)], [(w_lru_proj[0], 0)], [("pair",) + sg_rec],
                  [F32], _ep_gate_mul, n_cols=d)
    (mixed,) = mm("attn_proj", [o], [(w_attn_proj[0], 0)],
                  [("pair",) + sg_att, ("pair",) + y_rec], [BF16], _ep_gate_mul_add,
                  n_cols=d)
    (x,) = mm("out_proj", [mixed], [(w_out[0], 0)], [("pair",) + x], [F32], _ep_residual,
              n_cols=d)

    y = ffn(x, norm_ffn2[0], w_ffn2_gate[0], w_ffn2_up[0], w_ffn2_down[0])

    kv_shape = (1, -1, WINDOW, N_KV_HEADS, HEAD_DIM)
    n_conv = CONV_W - 1
    sample_conv = jnp.concatenate([state_conv[0][:, 1:], xr[1][:, None, :]], axis=1)
    return (y[0].reshape(1, t_p, d), y[1].reshape(t_s, 1, d),
            xr[0][t_p - n_conv:].reshape(1, 1, n_conv, d), h_last.reshape(1, 1, d),
            k32[0][t_p - WINDOW:].reshape(kv_shape), v32[0][t_p - WINDOW:].reshape(kv_shape),
            sample_conv[None], h_s[None],
            new_k.reshape(kv_shape), new_v.reshape(kv_shape))
```

```python
import functools
import math

import jax
import jax.numpy as jnp
from jax import lax
from jax.experimental import pallas as pl
from jax.experimental.pallas import tpu as pltpu

HEAD_DIM = 128
N_KV_HEADS = 8
WINDOW = 128
N_RG_BLOCKS = 16
CONV_W = 4
RG_C = 8.0
NORM_EPS = 1e-6
MACARON_WEIGHT = 0.5

V7X_VMEM_BYTES = 64 * 1024 * 1024
V7X_LANES = 128
V7X_SUBLANES = 8
V7X_MXU_DIM = 256
V7X_VMEM_RESERVE_BYTES = 4 * 1024 * 1024
V7X_VMEM_MAX_REQUEST_BYTES = V7X_VMEM_BYTES - 1024 * 1024

F32 = jnp.float32
BF16 = jnp.bfloat16


def _nbytes(shape, dtype):
    return math.prod(shape) * jnp.dtype(dtype).itemsize


def _vmem_limit(pipelined_bytes, resident_bytes):
    want = 2 * pipelined_bytes + resident_bytes + V7X_VMEM_RESERVE_BYTES
    return int(min(want, V7X_VMEM_MAX_REQUEST_BYTES))


def _params(n_grid, pipelined_bytes, resident_bytes):
    return pltpu.CompilerParams(
        dimension_semantics=("arbitrary",) * n_grid,
        vmem_limit_bytes=_vmem_limit(pipelined_bytes, resident_bytes))


def _shift_div(x, n):
    assert n & (n - 1) == 0
    return lax.shift_right_logical(x, jnp.int32(n.bit_length() - 1))


def _rms_kernel(x_ref, g_ref, o_ref):
    x = x_ref[...]
    ms = jnp.mean(x * x, axis=-1, keepdims=True)
    o_ref[...] = (x * lax.rsqrt(ms + NORM_EPS) * g_ref[...]).astype(o_ref.dtype)


def _rmsnorm(x, g):
    rows, d = x.shape
    rb = min(rows, 256)
    assert rows % rb == 0
    blk = _nbytes((rb, d), F32) + _nbytes((rb, d), BF16)
    return pl.pallas_call(
        _rms_kernel,
        grid=(rows // rb,),
        in_specs=[pl.BlockSpec((rb, d), lambda i: (i, 0)),
                  pl.BlockSpec((1, d), lambda i: (0, 0))],
        out_specs=pl.BlockSpec((rb, d), lambda i: (i, 0)),
        out_shape=jax.ShapeDtypeStruct((rows, d), BF16),
        compiler_params=_params(1, blk, 3 * _nbytes((rb, d), F32)),
        name="rmsnorm",
    )(x, g.reshape(1, d))


def _tile_maps(n_i, n_j):
    def prompt(i, j, *_):
        return (i, j)

    def sample(i, j, *_):
        return (0, jnp.where(i == n_i - 1, j, 0))

    return prompt, sample


def _emit_norm(x, gain, xb_ref, rstd_ref, ss_ref, j, n_j, d):
    xb_ref[...] = (x * gain).astype(xb_ref.dtype)
    sq = x * x
    part = sq[:, 0:V7X_LANES]
    for t in range(1, x.shape[1] // V7X_LANES):
        part = part + sq[:, t * V7X_LANES:(t + 1) * V7X_LANES]

    @pl.when(j == 0)
    def _():
        ss_ref[...] = part

    @pl.when(j != 0)
    def _():
        ss_ref[...] += part

    @pl.when(j == n_j - 1)
    def _():
        ms = jnp.sum(ss_ref[...], axis=-1, keepdims=True) / d
        rstd_ref[...] = jnp.broadcast_to(lax.rsqrt(ms + NORM_EPS), rstd_ref.shape)


def _mm_kernel(*refs, n_x, n_w, extra_kinds, n_out, epilogue, n_i, n_j, rowscale, norm_d):
    pos = 0
    xs = [(refs[pos + 2 * t], refs[pos + 2 * t + 1]) for t in range(n_x)]
    pos += 2 * n_x
    ws = list(refs[pos:pos + n_w])
    pos += n_w
    extras = []
    for kind in extra_kinds:
        if kind == "pair":
            extras.append((refs[pos], refs[pos + 1]))
            pos += 2
        else:
            extras.append((refs[pos], refs[pos]))
            pos += 1
    if rowscale:
        scale_refs = (refs[pos], refs[pos + 1])
        pos += 2
    if norm_d:
        gain_ref = refs[pos]
        pos += 1
    outs = [(refs[pos + 2 * t], refs[pos + 2 * t + 1]) for t in range(n_out)]
    pos += 2 * n_out
    if norm_d:
        xb_refs, rstd_refs, ss_refs = (tuple(refs[pos + 2 * t:pos + 2 * t + 2])
                                       for t in range(3))
    i = pl.program_id(0)
    j = pl.program_id(1)
    wb = [w[...].astype(BF16) for w in ws]

    def compute(which):
        accs = []
        for t in range(n_w):
            x = xs[t if n_x > 1 else 0][which][...]
            accs.append(jnp.dot(x, wb[t], preferred_element_type=F32))
        if rowscale:
            r = scale_refs[which][...]
            scale = jnp.concatenate([r] * (accs[0].shape[1] // V7X_LANES), axis=1)
            accs = [acc * scale for acc in accs]
        res = epilogue(accs, [e[which][...] for e in extras])
        for o, r in zip(outs, res):
            o[which][...] = r.astype(o[which].dtype)
        if norm_d:
            _emit_norm(res[0], gain_ref[...], xb_refs[which], rstd_refs[which],
                       ss_refs[which], j, n_j, norm_d)

    compute(0)

    @pl.when(i == n_i - 1)
    def _():
        compute(1)


def _norm_side_specs(t_p, t_s, n_cols, tm, tn, map_p, map_s):
    rows_p = pl.BlockSpec((tm, V7X_LANES), lambda i, *_: (i, 0))
    rows_s = pl.BlockSpec((t_s, V7X_LANES), lambda i, *_: (0, 0))
    specs = [pl.BlockSpec((tm, tn), map_p), pl.BlockSpec((t_s, tn), map_s), rows_p, rows_s]
    shapes = [jax.ShapeDtypeStruct((t_p, n_cols), BF16),
              jax.ShapeDtypeStruct((t_s, n_cols), BF16),
              jax.ShapeDtypeStruct((t_p, V7X_LANES), F32),
              jax.ShapeDtypeStruct((t_s, V7X_LANES), F32)]
    scratch = [pltpu.VMEM((tm, V7X_LANES), F32), pltpu.VMEM((t_s, V7X_LANES), F32)]
    nbytes = _nbytes((tm + t_s, tn), BF16) + 2 * _nbytes((tm + t_s, V7X_LANES), F32)
    return specs, shapes, scratch, nbytes


def _matmul(name, xs, ws, extras, out_dtypes, epilogue, *, n_cols, tm, tn, rowscale=None,
            norm_gain=None):
    t_p, k = xs[0][0].shape
    t_s = xs[0][1].shape[0]
    assert t_p % tm == 0 and n_cols % tn == 0
    n_i = t_p // tm
    n_j = n_cols // tn
    map_p, map_s = _tile_maps(n_i, n_j)

    in_specs, args = [], []
    pipelined = 0
    for w, col0 in ws:
        assert col0 % tn == 0 and w.shape[0] == k
        in_specs.append(pl.BlockSpec((k, tn), lambda i, j, c=col0 // tn: (0, j + c)))
        args.append(w)
        pipelined += _nbytes((k, tn), w.dtype)
    kinds = []
    for e in extras:
        kinds.append(e[0])
        if e[0] == "pair":
            in_specs += [pl.BlockSpec((tm, tn), map_p), pl.BlockSpec((t_s, tn), map_s)]
            args += [e[1], e[2]]
            pipelined += _nbytes((tm + t_s, tn), e[1].dtype)
        elif e[0] == "row":
            in_specs.append(pl.BlockSpec((1, tn), lambda i, j: (0, j)))
            args.append(e[1])
        else:
            assert e[0] == "const" and e[1].ndim == 2
            in_specs.append(pl.BlockSpec(e[1].shape, lambda i, j: (0, 0)))
            args.append(e[1])
    if rowscale is not None:
        in_specs += [pl.BlockSpec((tm, V7X_LANES), lambda i, j: (i, 0)),
                     pl.BlockSpec((t_s, V7X_LANES), lambda i, j: (0, 0))]
        args += list(rowscale)
        pipelined += _nbytes((tm + t_s, V7X_LANES), F32)
    if norm_gain is not None:
        in_specs.append(pl.BlockSpec((1, tn), lambda i, j: (0, j)))
        args.append(norm_gain)
    out_specs, out_shape, scratch = [], [], []
    for dt in out_dtypes:
        out_specs += [pl.BlockSpec((tm, tn), map_p), pl.BlockSpec((t_s, tn), map_s)]
        out_shape += [jax.ShapeDtypeStruct((t_p, n_cols), dt),
                      jax.ShapeDtypeStruct((t_s, n_cols), dt)]
        pipelined += _nbytes((tm + t_s, tn), dt)
    if norm_gain is not None:
        assert n_cols == norm_gain.shape[1]
        side = _norm_side_specs(t_p, t_s, n_cols, tm, tn, map_p, map_s)
        out_specs += side[0]
        out_shape += side[1]
        scratch = side[2]
        pipelined += side[3]
    resident = len(ws) * (_nbytes((k, tn), BF16) + 4 * _nbytes((tm, tn), F32))
    pipelined += len(xs) * _nbytes((t_s, k), BF16)
    x_bytes = len(xs) * _nbytes((tm, k), BF16)
    double_x = (2 * (pipelined + x_bytes) + resident + V7X_VMEM_RESERVE_BYTES
                <= V7X_VMEM_MAX_REQUEST_BYTES)
    if double_x:
        x_spec = pl.BlockSpec((tm, k), lambda i, j: (i, 0))
        pipelined += x_bytes
    else:
        x_spec = pl.BlockSpec((tm, k), lambda i, j: (i, 0), pipeline_mode=pl.Buffered(1))
        resident += x_bytes
    x_specs, x_args = [], []
    for x_p, x_s in xs:
        assert x_p.dtype == BF16 and x_s.dtype == BF16
        x_specs += [x_spec, pl.BlockSpec((t_s, k), lambda i, j: (0, 0))]
        x_args += [x_p, x_s]
    in_specs = x_specs + in_specs
    args = x_args + args
    res = pl.pallas_call(
        functools.partial(_mm_kernel, n_x=len(xs), n_w=len(ws), extra_kinds=tuple(kinds),
                          n_out=len(out_dtypes), epilogue=epilogue, n_i=n_i, n_j=n_j,
                          rowscale=rowscale is not None,
                          norm_d=n_cols if norm_gain is not None else 0),
        grid=(n_i, n_j),
        in_specs=in_specs,
        out_specs=out_specs,
        out_shape=out_shape,
        scratch_shapes=scratch,
        compiler_params=_params(2, pipelined, resident),
        name=name,
    )(*args)
    return [(res[2 * t], res[2 * t + 1]) for t in range(len(res) // 2)]


def _sigmoid(x):
    return 1.0 / (1.0 + jnp.exp(-x))


def _ep_swiglu(accs, extras):
    g, u = accs
    return [g * _sigmoid(g) * u]


def _ep_xr_gelu(accs, extras):
    x = accs[1]
    c = math.sqrt(2.0 / math.pi)
    return [accs[0], 0.5 * x * (1.0 + jnp.tanh(c * (x + 0.044715 * (x * x * x))))]


def _ep_sigmoid2(accs, extras):
    return [_sigmoid(accs[0]), _sigmoid(accs[1])]


def _head_rms(acc, gain):
    parts = []
    for h in range(acc.shape[1] // HEAD_DIM):
        y = acc[:, h * HEAD_DIM:(h + 1) * HEAD_DIM]
        ms = jnp.mean(y * y, axis=-1, keepdims=True)
        parts.append(y * lax.rsqrt(ms + NORM_EPS) * gain)
    return jnp.concatenate(parts, axis=1)


def _ep_head_rms(accs, extras):
    return [_head_rms(accs[0], extras[0])]


def _ep_kv(accs, extras):
    k = _head_rms(accs[0], extras[0])
    return [k, k, accs[1], accs[1]]


def _ep_gate_mul(accs, extras):
    return [extras[0] * accs[0]]


def _ep_gate_mul_add(accs, extras):
    return [extras[1] + extras[0] * accs[0]]


def _ep_residual(accs, extras):
    return [extras[0] + accs[0]]


def _down_kernel(*refs, n_i, n_j, k_half, norm_d):
    a_p, a_s, w_ref, r_p, r_s = refs[:5]
    pos = 5
    if norm_d:
        gain_ref = refs[pos]
        pos += 1
    o_p, o_s = refs[pos:pos + 2]
    if norm_d:
        xb_refs, rstd_refs, ss_refs = (tuple(refs[pos + 2 + 2 * t:pos + 4 + 2 * t])
                                       for t in range(3))
    i = pl.program_id(0)
    j = pl.program_id(1)
    kk = pl.program_id(2)
    w = w_ref[...].astype(BF16)
    k0 = pl.multiple_of(kk * k_half, V7X_LANES)

    def compute(which, a_ref, r_ref, o_ref):
        d = MACARON_WEIGHT * jnp.dot(a_ref[:, pl.ds(k0, k_half)], w,
                                     preferred_element_type=F32)

        @pl.when(kk == 0)
        def _():
            o_ref[...] = r_ref[...] + d

        @pl.when(kk != 0)
        def _():
            o = o_ref[...] + d
            o_ref[...] = o
            if norm_d:
                _emit_norm(o, gain_ref[...], xb_refs[which], rstd_refs[which],
                           ss_refs[which], j, n_j, norm_d)

    compute(0, a_p, r_p, o_p)

    @pl.when(i == n_i - 1)
    def _():
        compute(1, a_s, r_s, o_s)


def _ffn_down(a, w, res, norm_gain, *, tm, tn):
    a_p, a_s = a
    r_p, r_s = res
    t_p, k = a_p.shape
    t_s = a_s.shape[0]
    n_cols = w.shape[1]
    assert k % 2 == 0 and (k // 2) % V7X_LANES == 0
    k_half = k // 2
    n_i = t_p // tm
    n_j = n_cols // tn
    map_p, map_s = _tile_maps(n_i, n_j)
    tile_p = pl.BlockSpec((tm, tn), map_p)
    tile_s = pl.BlockSpec((t_s, tn), map_s)
    pipelined = (_nbytes((k_half, tn), F32) + 2 * _nbytes((tm + t_s, tn), F32)
                 + _nbytes((t_s, k), BF16))
    resident = (_nbytes((tm, k), BF16) + _nbytes((k_half, tn), BF16)
                + 2 * _nbytes((tm, tn), F32))
    in_specs = [
        pl.BlockSpec((tm, k), lambda i, j, kk: (i, 0), pipeline_mode=pl.Buffered(1)),
        pl.BlockSpec((t_s, k), lambda i, j, kk: (0, 0)),
        pl.BlockSpec((k_half, tn), lambda i, j, kk: (kk, j)),
        tile_p, tile_s]
    args = [a_p, a_s, w, r_p, r_s]
    out_specs = [tile_p, tile_s]
    out_shape = [jax.ShapeDtypeStruct((t_p, n_cols), F32),
                 jax.ShapeDtypeStruct((t_s, n_cols), F32)]
    scratch = []
    if norm_gain is not None:
        in_specs.append(pl.BlockSpec((1, tn), lambda i, j, kk: (0, j)))
        args.append(norm_gain)
        side = _norm_side_specs(t_p, t_s, n_cols, tm, tn, map_p, map_s)
        out_specs += side[0]
        out_shape += side[1]
        scratch = side[2]
        pipelined += side[3]
    res = pl.pallas_call(
        functools.partial(_down_kernel, n_i=n_i, n_j=n_j, k_half=k_half,
                          norm_d=n_cols if norm_gain is not None else 0),
        grid=(n_i, n_j, 2),
        in_specs=in_specs,
        out_specs=out_specs,
        out_shape=out_shape,
        scratch_shapes=scratch,
        compiler_params=_params(3, pipelined, resident),
        name="ffn_down",
    )(*args)
    return [(res[2 * t], res[2 * t + 1]) for t in range(len(res) // 2)]


def _log_sigmoid(x):
    return jnp.minimum(x, 0.0) - jnp.log1p(jnp.exp(-jnp.abs(x)))


def _rg_gate_block(xc, wa, wx, ba, bx, lam, first_pos):
    xb = xc.astype(BF16)
    r = _sigmoid(jnp.dot(xb, wa, preferred_element_type=F32) + ba)
    g = _sigmoid(jnp.dot(xb, wx, preferred_element_type=F32) + bx)
    a = jnp.exp(r * (RG_C * _log_sigmoid(lam)))
    v = 1.0 - a * a
    mult = jnp.where(v > 0.0, v * lax.rsqrt(v), 0.0)
    if first_pos is not None:
        mult = jnp.where(first_pos, 1.0, mult)
    return a, mult * g * xc


def _lru_prompt_kernel(xr_ref, gy_ref, cw_ref, cb_ref, wa_ref, wx_ref, ba_ref, bx_ref,
                       lam_ref, hg_ref, hlast_ref, x_tail, a_buf, b_buf, h_carry, *, tc):
    t = pl.program_id(0)
    d = xr_ref.shape[1]
    bw = d // N_RG_BLOCKS
    pad = V7X_SUBLANES

    @pl.when(t == 0)
    def _():
        x_tail[...] = jnp.zeros_like(x_tail)
        h_carry[...] = jnp.zeros_like(h_carry)

    row = lax.broadcasted_iota(jnp.int32, (tc, 1), 0)
    first_pos = (row + t * tc) == 0
    sub = lax.broadcasted_iota(jnp.int32, (tc // V7X_SUBLANES, V7X_SUBLANES, bw), 1)
    for c in range(N_RG_BLOCKS):
        sl = slice(c * bw, (c + 1) * bw)
        x = xr_ref[:, sl]
        x_ext = jnp.concatenate([x_tail[:, sl], x], axis=0)
        x_tail[:, sl] = x[tc - pad:, :]
        shifted = [x_ext]
        for _ in range(CONV_W - 1):
            shifted.append(pltpu.roll(shifted[-1], 1, axis=0))
        xc = cb_ref[:, sl]
        for j in range(CONV_W):
            back = CONV_W - 1 - j
            xs = x if back == 0 else shifted[back][pad:, :]
            xc = xc + cw_ref[j:j + 1, sl] * xs
        a, b = _rg_gate_block(xc, wa_ref[c], wx_ref[c], ba_ref[:, sl], bx_ref[:, sl],
                              lam_ref[:, sl], first_pos)
        a = a.reshape(tc // V7X_SUBLANES, V7X_SUBLANES, bw)
        b = b.reshape(tc // V7X_SUBLANES, V7X_SUBLANES, bw)
        for s in (1, 2, 4):
            keep = sub >= s
            a_sh = pltpu.roll(a, s, axis=1)
            b_sh = pltpu.roll(b, s, axis=1)
            b = jnp.where(keep, a * b_sh + b, b)
            a = jnp.where(keep, a * a_sh, a)
        a_buf[:, sl] = a.reshape(tc, bw)
        b_buf[:, sl] = b.reshape(tc, bw)

    def group(gi, h_prev):
        r0 = pl.multiple_of(gi * V7X_SUBLANES, V7X_SUBLANES)
        h = b_buf[pl.ds(r0, V7X_SUBLANES), :] + a_buf[pl.ds(r0, V7X_SUBLANES), :] * h_prev
        b_buf[pl.ds(r0, V7X_SUBLANES), :] = h
        return jnp.broadcast_to(h[V7X_SUBLANES - 1:V7X_SUBLANES, :], h.shape)

    h_last = lax.fori_loop(0, tc // V7X_SUBLANES, group, h_carry[...])
    h_carry[...] = h_last
    hlast_ref[...] = h_last[0:1, :]
    hg_ref[...] = (b_buf[...] * gy_ref[...]).astype(hg_ref.dtype)


def _lru_prompt(xr, gy, conv_w, conv_b, wa, wx, ba, bx, lam, *, tc):
    t_p, d = xr.shape
    bw = d // N_RG_BLOCKS
    row = pl.BlockSpec((1, d), lambda t: (0, 0))
    wspec = pl.BlockSpec((N_RG_BLOCKS, bw, bw), lambda t: (0, 0, 0))
    chunk = pl.BlockSpec((tc, d), lambda t: (t, 0))
    pipelined = (2 * _nbytes((tc, d), F32) + _nbytes((tc, d), BF16)
                 + 2 * _nbytes((N_RG_BLOCKS, bw, bw), BF16))
    scratch = [pltpu.VMEM((V7X_SUBLANES, d), F32), pltpu.VMEM((tc, d), F32),
               pltpu.VMEM((tc, d), F32), pltpu.VMEM((V7X_SUBLANES, d), F32)]
    resident = 4 * _nbytes((tc, d), F32)
    return pl.pallas_call(
        functools.partial(_lru_prompt_kernel, tc=tc),
        grid=(t_p // tc,),
        in_specs=[chunk, chunk, pl.BlockSpec((CONV_W, d), lambda t: (0, 0)), row,
                  wspec, wspec, row, row, row],
        out_specs=[chunk, row],
        out_shape=[jax.ShapeDtypeStruct((t_p, d), BF16),
                   jax.ShapeDtypeStruct((1, d), F32)],
        scratch_shapes=scratch,
        compiler_params=_params(1, pipelined, resident),
        name="lru_prompt",
    )(xr, gy, conv_w, conv_b, wa, wx, ba, bx, lam)


def _lru_sample_kernel(xr_ref, gy_ref, s0_ref, s1_ref, s2_ref, h0_ref, cw_ref, cb_ref,
                       wa_ref, wx_ref, ba_ref, bx_ref, lam_ref, hg_ref, h_ref):
    xc = cb_ref[...]
    for j, s_ref in enumerate((s0_ref, s1_ref, s2_ref)):
        xc = xc + cw_ref[j:j + 1, :] * s_ref[...]
    xc = xc + cw_ref[CONV_W - 1:CONV_W, :] * xr_ref[...]
    a, u = _rg_gate_block(xc, wa_ref[0], wx_ref[0], ba_ref[...], bx_ref[...],
                          lam_ref[...], None)
    h = u + a * h0_ref[...]
    h_ref[...] = h
    hg_ref[...] = (h * gy_ref[...]).astype(hg_ref.dtype)


def _lru_sample(xr, gy, state_conv, state_h, conv_w, conv_b, wa, wx, ba, bx, lam):
    t_s, d = xr.shape
    bw = d // N_RG_BLOCKS
    assert CONV_W == 4
    tile = pl.BlockSpec((t_s, bw), lambda c: (0, c))
    row = pl.BlockSpec((1, bw), lambda c: (0, c))
    wspec = pl.BlockSpec((1, bw, bw), lambda c: (c, 0, 0))
    state = [pl.BlockSpec((t_s, bw), lambda c, j=j: (0, j * N_RG_BLOCKS + c))
             for j in range(CONV_W - 1)]
    pipelined = 9 * _nbytes((t_s, bw), F32) + 2 * _nbytes((bw, bw), BF16)
    return pl.pallas_call(
        _lru_sample_kernel,
        grid=(N_RG_BLOCKS,),
        in_specs=[tile, tile] + state + [tile, pl.BlockSpec((CONV_W, bw), lambda c: (0, c)),
                                         row, wspec, wspec, row, row, row],
        out_specs=[tile, tile],
        out_shape=[jax.ShapeDtypeStruct((t_s, d), BF16),
                   jax.ShapeDtypeStruct((t_s, d), F32)],
        compiler_params=_params(1, pipelined, 8 * _nbytes((t_s, bw), F32)),
        name="lru_sample",
    )(xr, gy, state_conv, state_conv, state_conv, state_h, conv_w, conv_b, wa, wx,
      ba, bx, lam)


def _alibi_slope(head, n_heads):
    return 2.0 ** (-8.0 * (head + 1) / n_heads)


def _softmax_sink(s, sink):
    m = jnp.maximum(jnp.max(s, axis=-1, keepdims=True), sink)
    p = jnp.exp(s - m)
    return p, jnp.sum(p, axis=-1, keepdims=True) + jnp.exp(sink - m)


def _attn_prompt_kernel(sinks_ref, q_ref, kp_ref, kc_ref, vp_ref, vc_ref, o_ref, *,
                        n_heads):
    n = pl.program_id(0)
    blk = q_ref.shape[0]
    group = n_heads // N_KV_HEADS
    qi = lax.broadcasted_iota(jnp.int32, (blk, 2 * blk), 0)
    kj = lax.broadcasted_iota(jnp.int32, (blk, 2 * blk), 1)
    dist = qi + blk - kj
    exists = (kj + n * blk) >= blk
    valid = (dist >= 0) & (dist < WINDOW) & exists
    dist_f = dist.astype(F32)
    scale = HEAD_DIM ** -0.5
    for kv in range(N_KV_HEADS):
        cols = slice(kv * HEAD_DIM, (kv + 1) * HEAD_DIM)
        keys = jnp.concatenate([kp_ref[:, cols], kc_ref[:, cols]], axis=0)
        vals = jnp.concatenate([vp_ref[:, cols], vc_ref[:, cols]], axis=0)
        for g in range(group):
            head = kv * group + g
            hc = slice(head * HEAD_DIM, (head + 1) * HEAD_DIM)
            s = lax.dot_general(q_ref[:, hc], keys, (((1,), (1,)), ((), ())),
                                preferred_element_type=F32)
            s = s * scale - _alibi_slope(head, n_heads) * dist_f
            s = jnp.where(valid, s, -jnp.inf)
            sink = jnp.full((blk, 1), sinks_ref[head], F32)
            p, denom = _softmax_sink(s, sink)
            o = jnp.dot(p.astype(BF16), vals, preferred_element_type=F32)
            o_ref[:, hc] = (o / denom).astype(o_ref.dtype)


def _attn_prompt(q, k, v, sinks):
    t_p, dq = q.shape
    dk = k.shape[1]
    blk = WINDOW
    cur = lambda n: (n, 0)
    prev = lambda n: (jnp.maximum(n - 1, 0), 0)
    pipelined = 2 * _nbytes((blk, dq), BF16) + 4 * _nbytes((blk, dk), BF16)
    return pl.pallas_call(
        functools.partial(_attn_prompt_kernel, n_heads=dq // HEAD_DIM),
        grid=(t_p // blk,),
        in_specs=[pl.BlockSpec(memory_space=pltpu.SMEM),
                  pl.BlockSpec((blk, dq), cur),
                  pl.BlockSpec((blk, dk), prev), pl.BlockSpec((blk, dk), cur),
                  pl.BlockSpec((blk, dk), prev), pl.BlockSpec((blk, dk), cur)],
        out_specs=pl.BlockSpec((blk, dq), cur),
        out_shape=jax.ShapeDtypeStruct((t_p, dq), BF16),
        compiler_params=_params(1, pipelined, 16 * _nbytes((blk, 2 * blk), F32)),
        name="attn_prompt",
    )(sinks, q, k, k, v, v)


def _attn_sample_kernel(sink_ref, q_ref, kn_ref, vn_ref, ck_ref, cv_ref,
                        o_ref, ok_ref, ov_ref, *, n_heads, bb):
    group = n_heads // N_KV_HEADS
    dk = N_KV_HEADS * HEAD_DIM
    row_kv = _shift_div(lax.broadcasted_iota(jnp.int32, (n_heads, dk), 0), group)
    col_kv = _shift_div(lax.broadcasted_iota(jnp.int32, (n_heads, dk), 1), HEAD_DIM)
    diag = row_kv == col_kv
    slot = lax.broadcasted_iota(jnp.int32, (n_heads, WINDOW), 1)
    dist = jnp.where(slot == 0, 0, WINDOW - slot).astype(F32)
    head_col = lax.broadcasted_iota(jnp.int32, (n_heads, 1), 0)
    slope = jnp.exp2(-8.0 * (head_col + 1).astype(F32) / n_heads)
    head_kv = _shift_div(lax.broadcasted_iota(jnp.int32, (n_heads, HEAD_DIM), 0), group)
    sink = sink_ref[...]
    first_row = lax.broadcasted_iota(jnp.int32, (WINDOW, dk), 0) == 0
    scale = HEAD_DIM ** -0.5
    tail = WINDOW * N_KV_HEADS

    def gather(c_ref, b, new_row):
        per_kv = [c_ref[b, pl.ds(kv, WINDOW, stride=N_KV_HEADS), :]
                  for kv in range(N_KV_HEADS)]
        wide = jnp.concatenate(per_kv, axis=1)
        return jnp.where(first_row, new_row, wide).astype(BF16)

    for b in range(bb):
        keys = gather(ck_ref, b, kn_ref[b:b + 1, :])
        vals = gather(cv_ref, b, vn_ref[b:b + 1, :])
        q = q_ref[b]
        q_wide = jnp.where(diag, jnp.concatenate([q] * N_KV_HEADS, axis=1),
                           jnp.zeros((), BF16))
        s = lax.dot_general(q_wide, keys, (((1,), (1,)), ((), ())),
                            preferred_element_type=F32)
        s = s * scale - slope * dist
        p, denom = _softmax_sink(s, sink)
        o_wide = jnp.dot(p.astype(BF16), vals, preferred_element_type=F32)
        o = jnp.zeros((n_heads, HEAD_DIM), F32)
        for kv in range(N_KV_HEADS):
            o = o + jnp.where(head_kv == kv,
                              o_wide[:, kv * HEAD_DIM:(kv + 1) * HEAD_DIM], 0.0)
        o_ref[b] = (o / denom).astype(o_ref.dtype)
        for c_ref, n_ref, out_ref in ((ck_ref, kn_ref, ok_ref), (cv_ref, vn_ref, ov_ref)):
            out_ref[b, 0:tail - N_KV_HEADS, :] = c_ref[b, N_KV_HEADS:tail, :]
            for kv in range(N_KV_HEADS):
                r = tail - N_KV_HEADS + kv
                out_ref[b, r:r + 1, :] = n_ref[b:b + 1, kv * HEAD_DIM:(kv + 1) * HEAD_DIM]


def _attn_sample(q, k_new, v_new, cache_k, cache_v, sink_col, *, bb):
    t_s, n_heads, _ = q.shape
    dk = N_KV_HEADS * HEAD_DIM
    rows = WINDOW * N_KV_HEADS
    cache = pl.BlockSpec((bb, rows, HEAD_DIM), lambda i: (i, 0, 0))
    new = pl.BlockSpec((bb, dk), lambda i: (i, 0))
    qspec = pl.BlockSpec((bb, n_heads, HEAD_DIM), lambda i: (i, 0, 0))
    pipelined = (4 * _nbytes((bb, rows, HEAD_DIM), F32) + 2 * _nbytes((bb, dk), F32)
                 + 2 * _nbytes((bb, n_heads, HEAD_DIM), BF16))
    return pl.pallas_call(
        functools.partial(_attn_sample_kernel, n_heads=n_heads, bb=bb),
        grid=(t_s // bb,),
        in_specs=[pl.BlockSpec((n_heads, 1), lambda i: (0, 0)), qspec, new, new,
                  cache, cache],
        out_specs=[qspec, cache, cache],
        out_shape=[jax.ShapeDtypeStruct((t_s, n_heads, HEAD_DIM), BF16),
                   jax.ShapeDtypeStruct((t_s, rows, HEAD_DIM), F32),
                   jax.ShapeDtypeStruct((t_s, rows, HEAD_DIM), F32)],
        compiler_params=_params(1, pipelined, 8 * _nbytes((WINDOW, dk), F32)),
        name="attn_sample",
    )(sink_col, q, k_new, v_new, cache_k, cache_v)


def _swiglu_ffn(x, h, rstd, w_gate, w_up, w_down, next_gain, *, tm, tm_down, tn):
    d_ff = w_gate.shape[1]
    (a,) = _matmul("ffn_gate_up", [h], [(w_gate, 0), (w_up, 0)], [], [BF16], _ep_swiglu,
                   n_cols=d_ff, tm=tm, tn=tn, rowscale=rstd)
    return _ffn_down(a, w_down, x, next_gain, tm=tm_down, tn=tn)


def kernel(x_prompt, x_sample, state_conv, state_h, cache_k, cache_v, norm_ffn1, w_ffn1_gate, w_ffn1_up, w_ffn1_down, norm_mix, w_in, conv_w, conv_b, rg_w_a, rg_b_a, rg_w_x, rg_b_x, rg_lambda, q_norm, k_norm, sinks, w_lru_proj, w_attn_proj, w_out, norm_ffn2, w_ffn2_gate, w_ffn2_up, w_ffn2_down):
    b_p, t_p, d = x_prompt.shape
    t_s = x_sample.shape[0]
    depth = w_in.shape[0]
    assert b_p == 1 and x_sample.shape[1] == 1 and depth == 1
    n_heads = d // HEAD_DIM
    dk = N_KV_HEADS * HEAD_DIM
    tm, tm_down, tn = 2048, 1024, V7X_MXU_DIM
    mm = functools.partial(_matmul, tm=tm, tn=tn)
    ffn = functools.partial(_swiglu_ffn, tm=tm, tm_down=tm_down, tn=tn)

    x = (x_prompt.reshape(t_p, d), x_sample.reshape(t_s, d))
    h = tuple(_rmsnorm(xi, norm_ffn1[0]) for xi in x)
    x, xb, rstd = ffn(x, h, None, w_ffn1_gate[0], w_ffn1_up[0], w_ffn1_down[0],
                      norm_mix[0].reshape(1, d))

    w = w_in[0]
    xr, gy = mm("in_rec", [xb], [(w, 0), (w, d)], [], [F32, F32], _ep_xr_gelu, n_cols=d,
                rowscale=rstd)
    (q,) = mm("in_q", [xb], [(w, 2 * d)], [("const", q_norm[0].reshape(1, HEAD_DIM))],
              [BF16], _ep_head_rms, n_cols=d, rowscale=rstd)
    k32, kb, v32, vb = mm("in_kv", [xb], [(w, 3 * d), (w, 3 * d + dk)],
                          [("const", k_norm[0].reshape(1, HEAD_DIM))],
                          [F32, BF16, F32, BF16], _ep_kv, n_cols=dk, rowscale=rstd)
    sg_rec, sg_att = mm("in_gates", [xb], [(w, 3 * d + 2 * dk), (w, 4 * d + 2 * dk)], [],
                        [F32, F32], _ep_sigmoid2, n_cols=d, rowscale=rstd)

    cw, cb = conv_w[0], conv_b[0].reshape(1, d)
    wa, wx = rg_w_a[0].astype(BF16), rg_w_x[0].astype(BF16)
    ba, bx = rg_b_a[0].reshape(1, d), rg_b_x[0].reshape(1, d)
    lam = rg_lambda[0].reshape(1, d)
    hg_p, h_last = _lru_prompt(xr[0], gy[0], cw, cb, wa, wx, ba, bx, lam, tc=256)
    hg_s, h_s = _lru_sample(xr[1], gy[1], state_conv[0].reshape(t_s, (CONV_W - 1) * d),
                            state_h[0], cw, cb, wa, wx, ba, bx, lam)

    o_p = _attn_prompt(q[0], kb[0], vb[0], sinks[0])
    o_s, new_k, new_v = _attn_sample(
        q[1].reshape(t_s, n_heads, HEAD_DIM), k32[1], v32[1],
        cache_k[0].reshape(t_s, WINDOW * N_KV_HEADS, HEAD_DIM),
        cache_v[0].reshape(t_s, WINDOW * N_KV_HEADS, HEAD_DIM),
        sinks[0].reshape(n_heads, 1), bb=8)
    o = (o_p, o_s.reshape(t_s, d))

    (y_rec,) = mm("lru_proj", [(hg_p, hg_s)], [(w_lru_proj[0], 0)], [("pair",) + sg_rec],
                  [F32], _ep_gate_mul, n_cols=d)
    (mixed,) = mm("attn_proj", [o], [(w_attn_proj[0], 0)],
                  [("pair",) + sg_att, ("pair",) + y_rec], [BF16], _ep_gate_mul_add,
                  n_cols=d)
    x, xb, rstd = mm("out_proj", [mixed], [(w_out[0], 0)], [("pair",) + x], [F32],
                     _ep_residual, n_cols=d, norm_gain=norm_ffn2[0].reshape(1, d))

    (y,) = ffn(x, xb, rstd, w_ffn2_gate[0], w_ffn2_up[0], w_ffn2_down[0], None)

    kv_shape = (1, -1, WINDOW, N_KV_HEADS, HEAD_DIM)
    n_conv = CONV_W - 1
    sample_conv = jnp.concatenate([state_conv[0][:, 1:], xr[1][:, None, :]], axis=1)
    return (y[0].reshape(1, t_p, d), y[1].reshape(t_s, 1, d),
            xr[0][t_p - n_conv:].reshape(1, 1, n_conv, d), h_last.reshape(1, 1, d),
            k32[0][t_p - WINDOW:].reshape(kv_shape), v32[0][t_p - WINDOW:].reshape(kv_shape),
            sample_conv[None], h_s[None],
            new_k.reshape(kv_shape), new_v.reshape(kv_shape))
```

```python
import functools
import math

import jax
import jax.numpy as jnp
from jax import lax
from jax.experimental import pallas as pl
from jax.experimental.pallas import tpu as pltpu

HEAD_DIM = 128
N_KV_HEADS = 8
WINDOW = 128
N_RG_BLOCKS = 16
CONV_W = 4
RG_C = 8.0
NORM_EPS = 1e-6
MACARON_WEIGHT = 0.5

V7X_VMEM_BYTES = 64 * 1024 * 1024
V7X_LANES = 128
V7X_SUBLANES = 8
V7X_MXU_DIM = 256
V7X_VMEM_RESERVE_BYTES = 4 * 1024 * 1024
V7X_VMEM_MAX_REQUEST_BYTES = V7X_VMEM_BYTES - 1024 * 1024

F32 = jnp.float32
BF16 = jnp.bfloat16


def _nbytes(shape, dtype):
    return math.prod(shape) * jnp.dtype(dtype).itemsize


def _vmem_limit(pipelined_bytes, resident_bytes):
    want = 2 * pipelined_bytes + resident_bytes + V7X_VMEM_RESERVE_BYTES
    return int(min(want, V7X_VMEM_MAX_REQUEST_BYTES))


def _params(n_grid, pipelined_bytes, resident_bytes):
    return pltpu.CompilerParams(
        dimension_semantics=("arbitrary",) * n_grid,
        vmem_limit_bytes=_vmem_limit(pipelined_bytes, resident_bytes))


def _shift_div(x, n):
    assert n & (n - 1) == 0
    return lax.shift_right_logical(x, jnp.int32(n.bit_length() - 1))


def _rms_kernel(x_ref, g_ref, o_ref):
    x = x_ref[...]
    ms = jnp.mean(x * x, axis=-1, keepdims=True)
    o_ref[...] = (x * lax.rsqrt(ms + NORM_EPS) * g_ref[...]).astype(o_ref.dtype)


def _rmsnorm(x, g):
    rows, d = x.shape
    rb = min(rows, 256)
    assert rows % rb == 0
    blk = _nbytes((rb, d), F32) + _nbytes((rb, d), BF16)
    return pl.pallas_call(
        _rms_kernel,
        grid=(rows // rb,),
        in_specs=[pl.BlockSpec((rb, d), lambda i: (i, 0)),
                  pl.BlockSpec((1, d), lambda i: (0, 0))],
        out_specs=pl.BlockSpec((rb, d), lambda i: (i, 0)),
        out_shape=jax.ShapeDtypeStruct((rows, d), BF16),
        compiler_params=_params(1, blk, 3 * _nbytes((rb, d), F32)),
        name="rmsnorm",
    )(x, g.reshape(1, d))


def _tile_maps(n_i, n_j):
    def prompt(i, j, *_):
        return (i, j)

    def sample(i, j, *_):
        return (0, jnp.where(i == n_i - 1, j, 0))

    return prompt, sample


def _emit_norm(x, gain, xb_ref, rstd_ref, ss_ref, j, n_j, d):
    xb_ref[...] = (x * gain).astype(xb_ref.dtype)
    sq = x * x
    part = sq[:, 0:V7X_LANES]
    for t in range(1, x.shape[1] // V7X_LANES):
        part = part + sq[:, t * V7X_LANES:(t + 1) * V7X_LANES]

    @pl.when(j == 0)
    def _():
        ss_ref[...] = part

    @pl.when(j != 0)
    def _():
        ss_ref[...] += part

    @pl.when(j == n_j - 1)
    def _():
        ms = jnp.sum(ss_ref[...], axis=-1, keepdims=True) / d
        rstd_ref[...] = jnp.broadcast_to(lax.rsqrt(ms + NORM_EPS), rstd_ref.shape)


def _mm_kernel(*refs, n_x, n_w, extra_kinds, n_out, epilogue, n_i, n_j, rowscale, norm_d):
    pos = 0
    xs = [(refs[pos + 2 * t], refs[pos + 2 * t + 1]) for t in range(n_x)]
    pos += 2 * n_x
    ws = list(refs[pos:pos + n_w])
    pos += n_w
    extras = []
    for kind in extra_kinds:
        if kind == "pair":
            extras.append((refs[pos], refs[pos + 1]))
            pos += 2
        else:
            extras.append((refs[pos], refs[pos]))
            pos += 1
    if rowscale:
        scale_refs = (refs[pos], refs[pos + 1])
        pos += 2
    if norm_d:
        gain_ref = refs[pos]
        pos += 1
    outs = [(refs[pos + 2 * t], refs[pos + 2 * t + 1]) for t in range(n_out)]
    pos += 2 * n_out
    if norm_d:
        xb_refs, rstd_refs, ss_refs = (tuple(refs[pos + 2 * t:pos + 2 * t + 2])
                                       for t in range(3))
    i = pl.program_id(0)
    j = pl.program_id(1)
    wb = [w[...].astype(BF16) for w in ws]

    def compute(which):
        accs = []
        for t in range(n_w):
            x = xs[t if n_x > 1 else 0][which][...]
            accs.append(jnp.dot(x, wb[t], preferred_element_type=F32))
        if rowscale:
            r = scale_refs[which][...]
            scale = jnp.concatenate([r] * (accs[0].shape[1] // V7X_LANES), axis=1)
            accs = [acc * scale for acc in accs]
        res = epilogue(accs, [e[which][...] for e in extras])
        for o, r in zip(outs, res):
            o[which][...] = r.astype(o[which].dtype)
        if norm_d:
            _emit_norm(res[0], gain_ref[...], xb_refs[which], rstd_refs[which],
                       ss_refs[which], j, n_j, norm_d)

    compute(0)

    @pl.when(i == n_i - 1)
    def _():
        compute(1)


def _norm_side_specs(t_p, t_s, n_cols, tm, tn, map_p, map_s):
    rows_p = pl.BlockSpec((tm, V7X_LANES), lambda i, *_: (i, 0))
    rows_s = pl.BlockSpec((t_s, V7X_LANES), lambda i, *_: (0, 0))
    specs = [pl.BlockSpec((tm, tn), map_p), pl.BlockSpec((t_s, tn), map_s), rows_p, rows_s]
    shapes = [jax.ShapeDtypeStruct((t_p, n_cols), BF16),
              jax.ShapeDtypeStruct((t_s, n_cols), BF16),
              jax.ShapeDtypeStruct((t_p, V7X_LANES), F32),
              jax.ShapeDtypeStruct((t_s, V7X_LANES), F32)]
    scratch = [pltpu.VMEM((tm, V7X_LANES), F32), pltpu.VMEM((t_s, V7X_LANES), F32)]
    nbytes = _nbytes((tm + t_s, tn), BF16) + 2 * _nbytes((tm + t_s, V7X_LANES), F32)
    return specs, shapes, scratch, nbytes


def _matmul(name, xs, ws, extras, out_dtypes, epilogue, *, n_cols, tm, tn, rowscale=None,
            norm_gain=None):
    t_p = xs[0][0].shape[0]
    t_s = xs[0][1].shape[0]
    x_widths = [x_p.shape[1] for x_p, _ in xs]
    k = x_widths[0]
    assert len(xs) in (1, len(ws)) and all(kx == k for kx in x_widths)
    assert t_p % tm == 0 and n_cols % tn == 0
    n_i = t_p // tm
    n_j = n_cols // tn
    map_p, map_s = _tile_maps(n_i, n_j)

    in_specs, args = [], []
    pipelined = 0
    for w, col0 in ws:
        assert col0 % tn == 0 and w.shape[0] == k
        in_specs.append(pl.BlockSpec((k, tn), lambda i, j, c=col0 // tn: (0, j + c)))
        args.append(w)
        pipelined += _nbytes((k, tn), w.dtype)
    kinds = []
    for e in extras:
        kinds.append(e[0])
        if e[0] == "pair":
            in_specs += [pl.BlockSpec((tm, tn), map_p), pl.BlockSpec((t_s, tn), map_s)]
            args += [e[1], e[2]]
            pipelined += _nbytes((tm + t_s, tn), e[1].dtype)
        elif e[0] == "row":
            in_specs.append(pl.BlockSpec((1, tn), lambda i, j: (0, j)))
            args.append(e[1])
        else:
            assert e[0] == "const" and e[1].ndim == 2
            in_specs.append(pl.BlockSpec(e[1].shape, lambda i, j: (0, 0)))
            args.append(e[1])
    if rowscale is not None:
        in_specs += [pl.BlockSpec((tm, V7X_LANES), lambda i, j: (i, 0)),
                     pl.BlockSpec((t_s, V7X_LANES), lambda i, j: (0, 0))]
        args += list(rowscale)
        pipelined += _nbytes((tm + t_s, V7X_LANES), F32)
    if norm_gain is not None:
        in_specs.append(pl.BlockSpec((1, tn), lambda i, j: (0, j)))
        args.append(norm_gain)
    out_specs, out_shape, scratch = [], [], []
    for dt in out_dtypes:
        out_specs += [pl.BlockSpec((tm, tn), map_p), pl.BlockSpec((t_s, tn), map_s)]
        out_shape += [jax.ShapeDtypeStruct((t_p, n_cols), dt),
                      jax.ShapeDtypeStruct((t_s, n_cols), dt)]
        pipelined += _nbytes((tm + t_s, tn), dt)
    if norm_gain is not None:
        assert n_cols == norm_gain.shape[1]
        side = _norm_side_specs(t_p, t_s, n_cols, tm, tn, map_p, map_s)
        out_specs += side[0]
        out_shape += side[1]
        scratch = side[2]
        pipelined += side[3]
    resident = len(ws) * (_nbytes((k, tn), BF16) + 4 * _nbytes((tm, tn), F32))
    pipelined += _nbytes((t_s, sum(x_widths)), BF16)
    x_bytes = _nbytes((tm, sum(x_widths)), BF16)
    double_x = (2 * (pipelined + x_bytes) + resident + V7X_VMEM_RESERVE_BYTES
                <= V7X_VMEM_MAX_REQUEST_BYTES)
    if double_x:
        pipelined += x_bytes
    else:
        resident += x_bytes
    x_specs, x_args = [], []
    for (x_p, x_s), kx in zip(xs, x_widths):
        assert x_p.dtype == BF16 and x_s.dtype == BF16
        if double_x:
            x_spec = pl.BlockSpec((tm, kx), lambda i, j: (i, 0))
        else:
            x_spec = pl.BlockSpec((tm, kx), lambda i, j: (i, 0),
                                  pipeline_mode=pl.Buffered(1))
        x_specs += [x_spec, pl.BlockSpec((t_s, kx), lambda i, j: (0, 0))]
        x_args += [x_p, x_s]
    in_specs = x_specs + in_specs
    args = x_args + args
    res = pl.pallas_call(
        functools.partial(_mm_kernel, n_x=len(xs), n_w=len(ws), extra_kinds=tuple(kinds),
                          n_out=len(out_dtypes), epilogue=epilogue, n_i=n_i, n_j=n_j,
                          rowscale=rowscale is not None,
                          norm_d=n_cols if norm_gain is not None else 0),
        grid=(n_i, n_j),
        in_specs=in_specs,
        out_specs=out_specs,
        out_shape=out_shape,
        scratch_shapes=scratch,
        compiler_params=_params(2, pipelined, resident),
        name=name,
    )(*args)
    return [(res[2 * t], res[2 * t + 1]) for t in range(len(res) // 2)]


def _sigmoid(x):
    return 1.0 / (1.0 + jnp.exp(-x))


def _ep_swiglu(accs, extras):
    g, u = accs
    return [g * _sigmoid(g) * u]


def _ep_xr_gelu(accs, extras):
    x = accs[1]
    c = math.sqrt(2.0 / math.pi)
    return [accs[0], 0.5 * x * (1.0 + jnp.tanh(c * (x + 0.044715 * (x * x * x))))]


def _ep_sigmoid2(accs, extras):
    return [_sigmoid(accs[0]), _sigmoid(accs[1])]


def _head_rms(acc, gain):
    parts = []
    for h in range(acc.shape[1] // HEAD_DIM):
        y = acc[:, h * HEAD_DIM:(h + 1) * HEAD_DIM]
        ms = jnp.mean(y * y, axis=-1, keepdims=True)
        parts.append(y * lax.rsqrt(ms + NORM_EPS) * gain)
    return jnp.concatenate(parts, axis=1)


def _ep_head_rms2(accs, extras):
    return [_head_rms(accs[0], extras[0]), _head_rms(accs[1], extras[0])]


def _ep_kv(accs, extras):
    k = _head_rms(accs[0], extras[0])
    return [k, k, accs[1], accs[1]]


def _ep_gate_mul(accs, extras):
    return [extras[0] * accs[0]]


def _ep_gate_mul_add(accs, extras):
    return [extras[1] + extras[0] * accs[0]]


def _ep_residual(accs, extras):
    return [extras[0] + accs[0]]


def _down_kernel(*refs, n_i, n_j, k_half, norm_d):
    a_p, a_s, w_ref, r_p, r_s = refs[:5]
    pos = 5
    if norm_d:
        gain_ref = refs[pos]
        pos += 1
    o_p, o_s = refs[pos:pos + 2]
    if norm_d:
        xb_refs, rstd_refs, ss_refs = (tuple(refs[pos + 2 + 2 * t:pos + 4 + 2 * t])
                                       for t in range(3))
    i = pl.program_id(0)
    j = pl.program_id(1)
    kk = pl.program_id(2)
    w = w_ref[...].astype(BF16)
    k0 = pl.multiple_of(kk * k_half, V7X_LANES)

    def compute(which, a_ref, r_ref, o_ref):
        d = MACARON_WEIGHT * jnp.dot(a_ref[:, pl.ds(k0, k_half)], w,
                                     preferred_element_type=F32)

        @pl.when(kk == 0)
        def _():
            o_ref[...] = r_ref[...] + d

        @pl.when(kk != 0)
        def _():
            o = o_ref[...] + d
            o_ref[...] = o
            if norm_d:
                _emit_norm(o, gain_ref[...], xb_refs[which], rstd_refs[which],
                           ss_refs[which], j, n_j, norm_d)

    compute(0, a_p, r_p, o_p)

    @pl.when(i == n_i - 1)
    def _():
        compute(1, a_s, r_s, o_s)


def _ffn_down(a, w, res, norm_gain, *, tm, tn):
    a_p, a_s = a
    r_p, r_s = res
    t_p, k = a_p.shape
    t_s = a_s.shape[0]
    n_cols = w.shape[1]
    assert k % 2 == 0 and (k // 2) % V7X_LANES == 0
    k_half = k // 2
    n_i = t_p // tm
    n_j = n_cols // tn
    map_p, map_s = _tile_maps(n_i, n_j)
    tile_p = pl.BlockSpec((tm, tn), map_p)
    tile_s = pl.BlockSpec((t_s, tn), map_s)
    pipelined = (_nbytes((k_half, tn), F32) + 2 * _nbytes((tm + t_s, tn), F32)
                 + _nbytes((t_s, k), BF16))
    resident = (_nbytes((tm, k), BF16) + _nbytes((k_half, tn), BF16)
                + 2 * _nbytes((tm, tn), F32))
    in_specs = [
        pl.BlockSpec((tm, k), lambda i, j, kk: (i, 0), pipeline_mode=pl.Buffered(1)),
        pl.BlockSpec((t_s, k), lambda i, j, kk: (0, 0)),
        pl.BlockSpec((k_half, tn), lambda i, j, kk: (kk, j)),
        tile_p, tile_s]
    args = [a_p, a_s, w, r_p, r_s]
    out_specs = [tile_p, tile_s]
    out_shape = [jax.ShapeDtypeStruct((t_p, n_cols), F32),
                 jax.ShapeDtypeStruct((t_s, n_cols), F32)]
    scratch = []
    if norm_gain is not None:
        in_specs.append(pl.BlockSpec((1, tn), lambda i, j, kk: (0, j)))
        args.append(norm_gain)
        side = _norm_side_specs(t_p, t_s, n_cols, tm, tn, map_p, map_s)
        out_specs += side[0]
        out_shape += side[1]
        scratch = side[2]
        pipelined += side[3]
    res = pl.pallas_call(
        functools.partial(_down_kernel, n_i=n_i, n_j=n_j, k_half=k_half,
                          norm_d=n_cols if norm_gain is not None else 0),
        grid=(n_i, n_j, 2),
        in_specs=in_specs,
        out_specs=out_specs,
        out_shape=out_shape,
        scratch_shapes=scratch,
        compiler_params=_params(3, pipelined, resident),
        name="ffn_down",
    )(*args)
    return [(res[2 * t], res[2 * t + 1]) for t in range(len(res) // 2)]


def _log_sigmoid(x):
    return jnp.minimum(x, 0.0) - jnp.log1p(jnp.exp(-jnp.abs(x)))


def _rg_gate_block(xc, wa, wx, ba, bx, lam, first_pos):
    xb = xc.astype(BF16)
    r = _sigmoid(jnp.dot(xb, wa, preferred_element_type=F32) + ba)
    g = _sigmoid(jnp.dot(xb, wx, preferred_element_type=F32) + bx)
    a = jnp.exp(r * (RG_C * _log_sigmoid(lam)))
    v = 1.0 - a * a
    mult = jnp.where(v > 0.0, v * lax.rsqrt(v), 0.0)
    if first_pos is not None:
        mult = jnp.where(first_pos, 1.0, mult)
    return a, mult * g * xc


def _lru_prompt_kernel(xr_ref, gy_ref, cw_ref, cb_ref, wa_ref, wx_ref, ba_ref, bx_ref,
                       lam_ref, hg_ref, hlast_ref, x_tail, a_buf, b_buf, h_carry, *, tc):
    t = pl.program_id(0)
    d = xr_ref.shape[1]
    bw = d // N_RG_BLOCKS
    pad = V7X_SUBLANES

    @pl.when(t == 0)
    def _():
        x_tail[...] = jnp.zeros_like(x_tail)
        h_carry[...] = jnp.zeros_like(h_carry)

    row = lax.broadcasted_iota(jnp.int32, (tc, 1), 0)
    first_pos = (row + t * tc) == 0
    sub = lax.broadcasted_iota(jnp.int32, (1, V7X_SUBLANES, bw), 1)
    for c in range(N_RG_BLOCKS):
        sl = slice(c * bw, (c + 1) * bw)
        x = xr_ref[:, sl]
        x_ext = jnp.concatenate([x_tail[:, sl], x], axis=0)
        x_tail[:, sl] = x[tc - pad:, :]
        shifted = [x_ext]
        for _ in range(CONV_W - 1):
            shifted.append(pltpu.roll(shifted[-1], 1, axis=0))
        xc = cb_ref[:, sl]
        for j in range(CONV_W):
            back = CONV_W - 1 - j
            xs = x if back == 0 else shifted[back][pad:, :]
            xc = xc + cw_ref[j:j + 1, sl] * xs
        a, b = _rg_gate_block(xc, wa_ref[c], wx_ref[c], ba_ref[:, sl], bx_ref[:, sl],
                              lam_ref[:, sl], first_pos)
        a = a.reshape(tc // V7X_SUBLANES, V7X_SUBLANES, bw)
        b = b.reshape(tc // V7X_SUBLANES, V7X_SUBLANES, bw)
        for s in (1, 2, 4):
            keep = sub >= s
            a_sh = pltpu.roll(a, s, axis=1)
            b_sh = pltpu.roll(b, s, axis=1)
            b = jnp.where(keep, a * b_sh + b, b)
            a = jnp.where(keep, a * a_sh, a)
        a_buf[:, sl] = a.reshape(tc, bw)
        b_buf[:, sl] = b.reshape(tc, bw)

    def group(gi, h_prev):
        r0 = pl.multiple_of(gi * V7X_SUBLANES, V7X_SUBLANES)
        h = b_buf[pl.ds(r0, V7X_SUBLANES), :] + a_buf[pl.ds(r0, V7X_SUBLANES), :] * h_prev
        b_buf[pl.ds(r0, V7X_SUBLANES), :] = h
        return jnp.broadcast_to(h[V7X_SUBLANES - 1:V7X_SUBLANES, :], h.shape)

    h_last = lax.fori_loop(0, tc // V7X_SUBLANES, group, h_carry[...])
    h_carry[...] = h_last
    hlast_ref[...] = h_last[0:1, :]
    hg_ref[...] = (b_buf[...] * gy_ref[...]).astype(hg_ref.dtype)


def _lru_prompt(xr, gy, conv_w, conv_b, wa, wx, ba, bx, lam, *, tc):
    t_p, d = xr.shape
    bw = d // N_RG_BLOCKS
    row = pl.BlockSpec((1, d), lambda t: (0, 0))
    wspec = pl.BlockSpec((N_RG_BLOCKS, bw, bw), lambda t: (0, 0, 0))
    chunk = pl.BlockSpec((tc, d), lambda t: (t, 0))
    pipelined = (2 * _nbytes((tc, d), F32) + _nbytes((tc, d), BF16)
                 + 2 * _nbytes((N_RG_BLOCKS, bw, bw), BF16))
    scratch = [pltpu.VMEM((V7X_SUBLANES, d), F32), pltpu.VMEM((tc, d), F32),
               pltpu.VMEM((tc, d), F32), pltpu.VMEM((V7X_SUBLANES, d), F32)]
    resident = 4 * _nbytes((tc, d), F32)
    return pl.pallas_call(
        functools.partial(_lru_prompt_kernel, tc=tc),
        grid=(t_p // tc,),
        in_specs=[chunk, chunk, pl.BlockSpec((CONV_W, d), lambda t: (0, 0)), row,
                  wspec, wspec, row, row, row],
        out_specs=[chunk, row],
        out_shape=[jax.ShapeDtypeStruct((t_p, d), BF16),
                   jax.ShapeDtypeStruct((1, d), F32)],
        scratch_shapes=scratch,
        compiler_params=_params(1, pipelined, resident),
        name="lru_prompt",
    )(xr, gy, conv_w, conv_b, wa, wx, ba, bx, lam)


def _lru_sample_kernel(xr_ref, gy_ref, s0_ref, s1_ref, s2_ref, h0_ref, cw_ref, cb_ref,
                       wa_ref, wx_ref, ba_ref, bx_ref, lam_ref, hg_ref, h_ref):
    xc = cb_ref[...]
    for j, s_ref in enumerate((s0_ref, s1_ref, s2_ref)):
        xc = xc + cw_ref[j:j + 1, :] * s_ref[...]
    xc = xc + cw_ref[CONV_W - 1:CONV_W, :] * xr_ref[...]
    a, u = _rg_gate_block(xc, wa_ref[0], wx_ref[0], ba_ref[...], bx_ref[...],
                          lam_ref[...], None)
    h = u + a * h0_ref[...]
    h_ref[...] = h
    hg_ref[...] = (h * gy_ref[...]).astype(hg_ref.dtype)


def _lru_sample(xr, gy, state_conv, state_h, conv_w, conv_b, wa, wx, ba, bx, lam):
    t_s, d = xr.shape
    bw = d // N_RG_BLOCKS
    assert CONV_W == 4
    tile = pl.BlockSpec((t_s, bw), lambda c: (0, c))
    row = pl.BlockSpec((1, bw), lambda c: (0, c))
    wspec = pl.BlockSpec((1, bw, bw), lambda c: (c, 0, 0))
    state = [pl.BlockSpec((t_s, bw), lambda c, j=j: (0, j * N_RG_BLOCKS + c))
             for j in range(CONV_W - 1)]
    pipelined = 9 * _nbytes((t_s, bw), F32) + 2 * _nbytes((bw, bw), BF16)
    return pl.pallas_call(
        _lru_sample_kernel,
        grid=(N_RG_BLOCKS,),
        in_specs=[tile, tile] + state + [tile, pl.BlockSpec((CONV_W, bw), lambda c: (0, c)),
                                         row, wspec, wspec, row, row, row],
        out_specs=[tile, tile],
        out_shape=[jax.ShapeDtypeStruct((t_s, d), BF16),
                   jax.ShapeDtypeStruct((t_s, d), F32)],
        compiler_params=_params(1, pipelined, 8 * _nbytes((t_s, bw), F32)),
        name="lru_sample",
    )(xr, gy, state_conv, state_conv, state_conv, state_h, conv_w, conv_b, wa, wx,
      ba, bx, lam)


def _alibi_slope(head, n_heads):
    return 2.0 ** (-8.0 * (head + 1) / n_heads)


def _softmax_sink(s, sink):
    m = jnp.maximum(jnp.max(s, axis=-1, keepdims=True), sink)
    p = jnp.exp(s - m)
    return p, jnp.sum(p, axis=-1, keepdims=True) + jnp.exp(sink - m)


def _attn_prompt_kernel(*refs, n_heads, n_q):
    sinks_ref = refs[0]
    q_refs = refs[1:1 + n_q]
    kp_ref, kc_ref, vp_ref, vc_ref, o_ref, bias_ref = refs[1 + n_q:]
    n = pl.program_id(0)
    blk = o_ref.shape[0]
    group = n_heads // N_KV_HEADS
    heads_per_q = n_heads // n_q
    log2e = math.log2(math.e)

    @pl.when(n == 0)
    def _():
        qi = lax.broadcasted_iota(jnp.int32, (blk, 2 * blk), 0)
        kj = lax.broadcasted_iota(jnp.int32, (blk, 2 * blk), 1)
        dist = qi + blk - kj
        in_band = (dist >= 0) & (dist < WINDOW)
        dist_f = dist.astype(F32)
        for head in range(n_heads):
            term = jnp.where(in_band, (_alibi_slope(head, n_heads) * log2e) * dist_f, jnp.inf)
            bias_ref[1, head] = term
            bias_ref[0, head] = jnp.where(kj >= blk, term, jnp.inf)

    which = jnp.minimum(n, 1)
    scale = HEAD_DIM ** -0.5 * log2e
    for kv in range(N_KV_HEADS):
        cols = slice(kv * HEAD_DIM, (kv + 1) * HEAD_DIM)
        keys = jnp.concatenate([kp_ref[:, cols], kc_ref[:, cols]], axis=0)
        vals = jnp.concatenate([vp_ref[:, cols], vc_ref[:, cols]], axis=0)
        for g in range(group):
            head = kv * group + g
            q_ref = q_refs[head // heads_per_q]
            local = head % heads_per_q
            q = q_ref[:, local * HEAD_DIM:(local + 1) * HEAD_DIM]
            s = lax.dot_general(q, keys, (((1,), (1,)), ((), ())),
                                preferred_element_type=F32)
            s = s * scale - bias_ref[which, head]
            sink = jnp.full((blk, 1), sinks_ref[head] * log2e, F32)
            m = jnp.maximum(jnp.max(s, axis=-1, keepdims=True), sink)
            p = jnp.exp2(s - m)
            denom = jnp.sum(p, axis=-1, keepdims=True) + jnp.exp2(sink - m)
            o = jnp.dot(p.astype(BF16), vals, preferred_element_type=F32)
            o_ref[:, head * HEAD_DIM:(head + 1) * HEAD_DIM] = (o / denom).astype(o_ref.dtype)


def _attn_prompt(qs, k, v, sinks):
    t_p = k.shape[0]
    dq = sum(q.shape[1] for q in qs)
    dk = k.shape[1]
    blk = WINDOW
    n_heads = dq // HEAD_DIM
    cur = lambda n: (n, 0)
    prev = lambda n: (jnp.maximum(n - 1, 0), 0)
    pipelined = 2 * _nbytes((blk, dq), BF16) + 4 * _nbytes((blk, dk), BF16)
    bias_shape = (2, n_heads, blk, 2 * blk)
    return pl.pallas_call(
        functools.partial(_attn_prompt_kernel, n_heads=n_heads, n_q=len(qs)),
        grid=(t_p // blk,),
        in_specs=[pl.BlockSpec(memory_space=pltpu.SMEM)]
        + [pl.BlockSpec((blk, q.shape[1]), cur) for q in qs]
        + [pl.BlockSpec((blk, dk), prev), pl.BlockSpec((blk, dk), cur),
           pl.BlockSpec((blk, dk), prev), pl.BlockSpec((blk, dk), cur)],
        out_specs=pl.BlockSpec((blk, dq), cur),
        out_shape=jax.ShapeDtypeStruct((t_p, dq), BF16),
        scratch_shapes=[pltpu.VMEM(bias_shape, F32)],
        compiler_params=_params(1, pipelined, _nbytes(bias_shape, F32)
                                + 16 * _nbytes((blk, 2 * blk), F32)),
        name="attn_prompt",
    )(sinks, *qs, k, k, v, v)


def _attn_sample_kernel(sink_ref, q_ref, kn_ref, vn_ref, ck_ref, cv_ref,
                        o_ref, ok_ref, ov_ref, *, n_heads, bb):
    group = n_heads // N_KV_HEADS
    rows = WINDOW * N_KV_HEADS
    col = lax.broadcasted_iota(jnp.int32, (n_heads, rows), 1)
    head = lax.broadcasted_iota(jnp.int32, (n_heads, rows), 0)
    own_kv = jnp.bitwise_and(col, N_KV_HEADS - 1) == _shift_div(head, group)
    dist = (WINDOW - 1 - _shift_div(col, N_KV_HEADS)).astype(F32)
    slope = jnp.exp2(-8.0 * (head + 1).astype(F32) / n_heads)
    bias = jnp.where(own_kv, slope * dist, jnp.inf)
    sink = sink_ref[...]
    scale = HEAD_DIM ** -0.5

    for b in range(bb):
        for c_ref, n_ref, out_ref in ((ck_ref, kn_ref, ok_ref), (cv_ref, vn_ref, ov_ref)):
            out_ref[b, 0:rows - N_KV_HEADS, :] = c_ref[b, N_KV_HEADS:rows, :]
            for kv in range(N_KV_HEADS):
                r = rows - N_KV_HEADS + kv
                out_ref[b, r:r + 1, :] = n_ref[b:b + 1, kv * HEAD_DIM:(kv + 1) * HEAD_DIM]
        keys = ok_ref[b].astype(BF16)
        vals = ov_ref[b].astype(BF16)
        s = lax.dot_general(q_ref[b], keys, (((1,), (1,)), ((), ())),
                            preferred_element_type=F32)
        p, denom = _softmax_sink(s * scale - bias, sink)
        o = jnp.dot(p.astype(BF16), vals, preferred_element_type=F32)
        o_ref[b] = (o / denom).astype(o_ref.dtype)


def _attn_sample(q, k_new, v_new, cache_k, cache_v, sink_col, *, bb):
    t_s, n_heads, _ = q.shape
    dk = N_KV_HEADS * HEAD_DIM
    rows = WINDOW * N_KV_HEADS
    cache = pl.BlockSpec((bb, rows, HEAD_DIM), lambda i: (i, 0, 0))
    new = pl.BlockSpec((bb, dk), lambda i: (i, 0))
    qspec = pl.BlockSpec((bb, n_heads, HEAD_DIM), lambda i: (i, 0, 0))
    pipelined = (4 * _nbytes((bb, rows, HEAD_DIM), F32) + 2 * _nbytes((bb, dk), F32)
                 + 2 * _nbytes((bb, n_heads, HEAD_DIM), BF16))
    return pl.pallas_call(
        functools.partial(_attn_sample_kernel, n_heads=n_heads, bb=bb),
        grid=(t_s // bb,),
        in_specs=[pl.BlockSpec((n_heads, 1), lambda i: (0, 0)), qspec, new, new,
                  cache, cache],
        out_specs=[qspec, cache, cache],
        out_shape=[jax.ShapeDtypeStruct((t_s, n_heads, HEAD_DIM), BF16),
                   jax.ShapeDtypeStruct((t_s, rows, HEAD_DIM), F32),
                   jax.ShapeDtypeStruct((t_s, rows, HEAD_DIM), F32)],
        compiler_params=_params(1, pipelined, 8 * _nbytes((WINDOW, dk), F32)),
        name="attn_sample",
    )(sink_col, q, k_new, v_new, cache_k, cache_v)


def _swiglu_ffn(x, h, rstd, w_gate, w_up, w_down, next_gain, *, tm, tm_down, tn):
    d_ff = w_gate.shape[1]
    (a,) = _matmul("ffn_gate_up", [h], [(w_gate, 0), (w_up, 0)], [], [BF16], _ep_swiglu,
                   n_cols=d_ff, tm=tm, tn=tn, rowscale=rstd)
    return _ffn_down(a, w_down, x, next_gain, tm=tm_down, tn=tn)


def kernel(x_prompt, x_sample, state_conv, state_h, cache_k, cache_v, norm_ffn1, w_ffn1_gate, w_ffn1_up, w_ffn1_down, norm_mix, w_in, conv_w, conv_b, rg_w_a, rg_b_a, rg_w_x, rg_b_x, rg_lambda, q_norm, k_norm, sinks, w_lru_proj, w_attn_proj, w_out, norm_ffn2, w_ffn2_gate, w_ffn2_up, w_ffn2_down):
    b_p, t_p, d = x_prompt.shape
    t_s = x_sample.shape[0]
    depth = w_in.shape[0]
    assert b_p == 1 and x_sample.shape[1] == 1 and depth == 1
    n_heads = d // HEAD_DIM
    dk = N_KV_HEADS * HEAD_DIM
    tm, tm_down, tn = 2048, 1024, V7X_MXU_DIM
    mm = functools.partial(_matmul, tm=tm, tn=tn)
    ffn = functools.partial(_swiglu_ffn, tm=tm, tm_down=tm_down, tn=tn)

    x = (x_prompt.reshape(t_p, d), x_sample.reshape(t_s, d))
    h = tuple(_rmsnorm(xi, norm_ffn1[0]) for xi in x)
    x, xb, rstd = ffn(x, h, None, w_ffn1_gate[0], w_ffn1_up[0], w_ffn1_down[0],
                      norm_mix[0].reshape(1, d))

    w = w_in[0]
    xr, gy = mm("in_rec", [xb], [(w, 0), (w, d)], [], [F32, F32], _ep_xr_gelu, n_cols=d,
                rowscale=rstd)
    q_lo, q_hi = mm("in_q", [xb], [(w, 2 * d), (w, 2 * d + d // 2)],
                    [("const", q_norm[0].reshape(1, HEAD_DIM))], [BF16, BF16],
                    _ep_head_rms2, n_cols=d // 2, rowscale=rstd)
    k32, kb, v32, vb = mm("in_kv", [xb], [(w, 3 * d), (w, 3 * d + dk)],
                          [("const", k_norm[0].reshape(1, HEAD_DIM))],
                          [F32, BF16, F32, BF16], _ep_kv, n_cols=dk, rowscale=rstd)
    sg_rec, sg_att = mm("in_gates", [xb], [(w, 3 * d + 2 * dk), (w, 4 * d + 2 * dk)], [],
                        [F32, F32], _ep_sigmoid2, n_cols=d, rowscale=rstd)

    cw, cb = conv_w[0], conv_b[0].reshape(1, d)
    wa, wx = rg_w_a[0].astype(BF16), rg_w_x[0].astype(BF16)
    ba, bx = rg_b_a[0].reshape(1, d), rg_b_x[0].reshape(1, d)
    lam = rg_lambda[0].reshape(1, d)
    hg_p, h_last = _lru_prompt(xr[0], gy[0], cw, cb, wa, wx, ba, bx, lam, tc=256)
    hg_s, h_s = _lru_sample(xr[1], gy[1], state_conv[0].reshape(t_s, (CONV_W - 1) * d),
                            state_h[0], cw, cb, wa, wx, ba, bx, lam)

    o_p = _attn_prompt([q_lo[0], q_hi[0]], kb[0], vb[0], sinks[0])
    q_s = jnp.concatenate([q_lo[1], q_hi[1]], axis=1)
    o_s, new_k, new_v = _attn_sample(
        q_s.reshape(t_s, n_heads, HEAD_DIM), k32[1], v32[1],
        cache_k[0].reshape(t_s, WINDOW * N_KV_HEADS, HEAD_DIM),
        cache_v[0].reshape(t_s, WINDOW * N_KV_HEADS, HEAD_DIM),
        sinks[0].reshape(n_heads, 1), bb=8)
    o = (o_p, o_s.reshape(t_s, d))

    (y_rec,) = mm("lru_proj", [(hg_p, hg_s)], [(w_lru_proj[0], 0)], [("pair",) + sg_rec],
                  [F32], _ep_gate_mul, n_cols=d)
    (mixed,) = mm("attn_proj", [o], [(w_attn_proj[0], 0)],
                  [("pair",) + sg_att, ("pair",) + y_rec], [BF16], _ep_gate_mul_add,
                  n_cols=d)
    x, xb, rstd = mm("out_proj", [mixed], [(w_out[0], 0)], [("pair",) + x], [F32],
                     _ep_residual, n_cols=d, norm_gain=norm_ffn2[0].reshape(1, d))

    (y,) = ffn(x, xb, rstd, w_ffn2_gate[0], w_ffn2_up[0], w_ffn2_down[0], None)

    kv_shape = (1, -1, WINDOW, N_KV_HEADS, HEAD_DIM)
    n_conv = CONV_W - 1
    sample_conv = jnp.concatenate([state_conv[0][:, 1:], xr[1][:, None, :]], axis=1)
    return (y[0].reshape(1, t_p, d), y[1].reshape(t_s, 1, d),
            xr[0][t_p - n_conv:].reshape(1, 1, n_conv, d), h_last.reshape(1, 1, d),
            k32[0][t_p - WINDOW:].reshape(kv_shape), v32[0][t_p - WINDOW:].reshape(kv_shape),
            sample_conv[None], h_s[None],
            new_k.reshape(kv_shape), new_v.reshape(kv_shape))
```

```python
import functools
import math

import jax
import jax.numpy as jnp
from jax import lax
from jax.experimental import pallas as pl
from jax.experimental.pallas import tpu as pltpu

HEAD_DIM = 128
N_KV_HEADS = 8
WINDOW = 128
N_RG_BLOCKS = 16
CONV_W = 4
RG_C = 8.0
NORM_EPS = 1e-6
MACARON_WEIGHT = 0.5

V7X_VMEM_BYTES = 64 * 1024 * 1024
V7X_LANES = 128
V7X_SUBLANES = 8
V7X_MXU_DIM = 256
V7X_VMEM_RESERVE_BYTES = 4 * 1024 * 1024
V7X_VMEM_MAX_REQUEST_BYTES = V7X_VMEM_BYTES - 1024 * 1024

F32 = jnp.float32
BF16 = jnp.bfloat16


def _nbytes(shape, dtype):
    return math.prod(shape) * jnp.dtype(dtype).itemsize


def _vmem_limit(pipelined_bytes, resident_bytes):
    want = 2 * pipelined_bytes + resident_bytes + V7X_VMEM_RESERVE_BYTES
    return int(min(want, V7X_VMEM_MAX_REQUEST_BYTES))


def _params(n_grid, pipelined_bytes, resident_bytes):
    return pltpu.CompilerParams(
        dimension_semantics=("arbitrary",) * n_grid,
        vmem_limit_bytes=_vmem_limit(pipelined_bytes, resident_bytes))


def _shift_div(x, n):
    assert n & (n - 1) == 0
    return lax.shift_right_logical(x, jnp.int32(n.bit_length() - 1))


def _rms_kernel(x_ref, g_ref, o_ref):
    x = x_ref[...]
    ms = jnp.mean(x * x, axis=-1, keepdims=True)
    o_ref[...] = (x * lax.rsqrt(ms + NORM_EPS) * g_ref[...]).astype(o_ref.dtype)


def _rmsnorm(x, g):
    rows, d = x.shape
    rb = min(rows, 256)
    assert rows % rb == 0
    blk = _nbytes((rb, d), F32) + _nbytes((rb, d), BF16)
    return pl.pallas_call(
        _rms_kernel,
        grid=(rows // rb,),
        in_specs=[pl.BlockSpec((rb, d), lambda i: (i, 0)),
                  pl.BlockSpec((1, d), lambda i: (0, 0))],
        out_specs=pl.BlockSpec((rb, d), lambda i: (i, 0)),
        out_shape=jax.ShapeDtypeStruct((rows, d), BF16),
        compiler_params=_params(1, blk, 3 * _nbytes((rb, d), F32)),
        name="rmsnorm",
    )(x, g.reshape(1, d))


def _tile_maps(n_i, n_j):
    def prompt(i, j, *_):
        return (i, j)

    def sample(i, j, *_):
        return (0, jnp.where(i == n_i - 1, j, 0))

    return prompt, sample


def _emit_norm(x, gain, xb_ref, rstd_ref, ss_ref, j, n_j, d):
    xb_ref[...] = (x * gain).astype(xb_ref.dtype)
    sq = x * x
    part = sq[:, 0:V7X_LANES]
    for t in range(1, x.shape[1] // V7X_LANES):
        part = part + sq[:, t * V7X_LANES:(t + 1) * V7X_LANES]

    @pl.when(j == 0)
    def _():
        ss_ref[...] = part

    @pl.when(j != 0)
    def _():
        ss_ref[...] += part

    @pl.when(j == n_j - 1)
    def _():
        ms = jnp.sum(ss_ref[...], axis=-1, keepdims=True) / d
        rstd_ref[...] = jnp.broadcast_to(lax.rsqrt(ms + NORM_EPS), rstd_ref.shape)


def _mm_kernel(*refs, n_x, n_w, extra_kinds, n_out, epilogue, n_i, n_j, rowscale, norm_d):
    pos = 0
    xs = [(refs[pos + 2 * t], refs[pos + 2 * t + 1]) for t in range(n_x)]
    pos += 2 * n_x
    ws = list(refs[pos:pos + n_w])
    pos += n_w
    extras = []
    for kind in extra_kinds:
        if kind == "pair":
            extras.append((refs[pos], refs[pos + 1]))
            pos += 2
        else:
            extras.append((refs[pos], refs[pos]))
            pos += 1
    if rowscale:
        scale_refs = (refs[pos], refs[pos + 1])
        pos += 2
    if norm_d:
        gain_ref = refs[pos]
        pos += 1
    outs = [(refs[pos + 2 * t], refs[pos + 2 * t + 1]) for t in range(n_out)]
    pos += 2 * n_out
    if norm_d:
        xb_refs, rstd_refs, ss_refs = (tuple(refs[pos + 2 * t:pos + 2 * t + 2])
                                       for t in range(3))
    i = pl.program_id(0)
    j = pl.program_id(1)
    wb = [w[...].astype(BF16) for w in ws]

    def compute(which):
        accs = []
        for t in range(n_w):
            x = xs[t if n_x > 1 else 0][which][...]
            accs.append(jnp.dot(x, wb[t], preferred_element_type=F32))
        if rowscale:
            r = scale_refs[which][...]
            scale = jnp.concatenate([r] * (accs[0].shape[1] // V7X_LANES), axis=1)
            accs = [acc * scale for acc in accs]
        res = epilogue(accs, [e[which][...] for e in extras])
        for o, r in zip(outs, res):
            o[which][...] = r.astype(o[which].dtype)
        if norm_d:
            _emit_norm(res[0], gain_ref[...], xb_refs[which], rstd_refs[which],
                       ss_refs[which], j, n_j, norm_d)

    compute(0)

    @pl.when(i == n_i - 1)
    def _():
        compute(1)


def _norm_side_specs(t_p, t_s, n_cols, tm, tn, map_p, map_s):
    rows_p = pl.BlockSpec((tm, V7X_LANES), lambda i, *_: (i, 0))
    rows_s = pl.BlockSpec((t_s, V7X_LANES), lambda i, *_: (0, 0))
    specs = [pl.BlockSpec((tm, tn), map_p), pl.BlockSpec((t_s, tn), map_s), rows_p, rows_s]
    shapes = [jax.ShapeDtypeStruct((t_p, n_cols), BF16),
              jax.ShapeDtypeStruct((t_s, n_cols), BF16),
              jax.ShapeDtypeStruct((t_p, V7X_LANES), F32),
              jax.ShapeDtypeStruct((t_s, V7X_LANES), F32)]
    scratch = [pltpu.VMEM((tm, V7X_LANES), F32), pltpu.VMEM((t_s, V7X_LANES), F32)]
    nbytes = _nbytes((tm + t_s, tn), BF16) + 2 * _nbytes((tm + t_s, V7X_LANES), F32)
    return specs, shapes, scratch, nbytes


def _matmul(name, xs, ws, extras, out_dtypes, epilogue, *, n_cols, tm, tn, rowscale=None,
            norm_gain=None):
    t_p = xs[0][0].shape[0]
    t_s = xs[0][1].shape[0]
    x_widths = [x_p.shape[1] for x_p, _ in xs]
    k = x_widths[0]
    assert len(xs) in (1, len(ws)) and all(kx == k for kx in x_widths)
    assert t_p % tm == 0 and n_cols % tn == 0
    n_i = t_p // tm
    n_j = n_cols // tn
    map_p, map_s = _tile_maps(n_i, n_j)

    in_specs, args = [], []
    pipelined = 0
    for w, col0 in ws:
        assert col0 % tn == 0 and w.shape[0] == k
        in_specs.append(pl.BlockSpec((k, tn), lambda i, j, c=col0 // tn: (0, j + c)))
        args.append(w)
        pipelined += _nbytes((k, tn), w.dtype)
    kinds = []
    for e in extras:
        kinds.append(e[0])
        if e[0] == "pair":
            in_specs += [pl.BlockSpec((tm, tn), map_p), pl.BlockSpec((t_s, tn), map_s)]
            args += [e[1], e[2]]
            pipelined += _nbytes((tm + t_s, tn), e[1].dtype)
        elif e[0] == "row":
            in_specs.append(pl.BlockSpec((1, tn), lambda i, j: (0, j)))
            args.append(e[1])
        else:
            assert e[0] == "const" and e[1].ndim == 2
            in_specs.append(pl.BlockSpec(e[1].shape, lambda i, j: (0, 0)))
            args.append(e[1])
    if rowscale is not None:
        in_specs += [pl.BlockSpec((tm, V7X_LANES), lambda i, j: (i, 0)),
                     pl.BlockSpec((t_s, V7X_LANES), lambda i, j: (0, 0))]
        args += list(rowscale)
        pipelined += _nbytes((tm + t_s, V7X_LANES), F32)
    if norm_gain is not None:
        in_specs.append(pl.BlockSpec((1, tn), lambda i, j: (0, j)))
        args.append(norm_gain)
    out_specs, out_shape, scratch = [], [], []
    for dt in out_dtypes:
        out_specs += [pl.BlockSpec((tm, tn), map_p), pl.BlockSpec((t_s, tn), map_s)]
        out_shape += [jax.ShapeDtypeStruct((t_p, n_cols), dt),
                      jax.ShapeDtypeStruct((t_s, n_cols), dt)]
        pipelined += _nbytes((tm + t_s, tn), dt)
    if norm_gain is not None:
        assert n_cols == norm_gain.shape[1]
        side = _norm_side_specs(t_p, t_s, n_cols, tm, tn, map_p, map_s)
        out_specs += side[0]
        out_shape += side[1]
        scratch = side[2]
        pipelined += side[3]
    resident = len(ws) * (_nbytes((k, tn), BF16) + 4 * _nbytes((tm, tn), F32))
    pipelined += _nbytes((t_s, sum(x_widths)), BF16)
    x_bytes = _nbytes((tm, sum(x_widths)), BF16)
    double_x = (2 * (pipelined + x_bytes) + resident + V7X_VMEM_RESERVE_BYTES
                <= V7X_VMEM_MAX_REQUEST_BYTES)
    if double_x:
        pipelined += x_bytes
    else:
        resident += x_bytes
    x_specs, x_args = [], []
    for (x_p, x_s), kx in zip(xs, x_widths):
        assert x_p.dtype == BF16 and x_s.dtype == BF16
        if double_x:
            x_spec = pl.BlockSpec((tm, kx), lambda i, j: (i, 0))
        else:
            x_spec = pl.BlockSpec((tm, kx), lambda i, j: (i, 0),
                                  pipeline_mode=pl.Buffered(1))
        x_specs += [x_spec, pl.BlockSpec((t_s, kx), lambda i, j: (0, 0))]
        x_args += [x_p, x_s]
    in_specs = x_specs + in_specs
    args = x_args + args
    res = pl.pallas_call(
        functools.partial(_mm_kernel, n_x=len(xs), n_w=len(ws), extra_kinds=tuple(kinds),
                          n_out=len(out_dtypes), epilogue=epilogue, n_i=n_i, n_j=n_j,
                          rowscale=rowscale is not None,
                          norm_d=n_cols if norm_gain is not None else 0),
        grid=(n_i, n_j),
        in_specs=in_specs,
        out_specs=out_specs,
        out_shape=out_shape,
        scratch_shapes=scratch,
        compiler_params=_params(2, pipelined, resident),
        name=name,
    )(*args)
    return [(res[2 * t], res[2 * t + 1]) for t in range(len(res) // 2)]


def _sigmoid(x):
    return 1.0 / (1.0 + jnp.exp(-x))


def _ep_swiglu(accs, extras):
    g, u = accs
    return [g * _sigmoid(g) * u]


def _ep_xr_gelu(accs, extras):
    x = accs[1]
    c = math.sqrt(2.0 / math.pi)
    return [accs[0], 0.5 * x * (1.0 + jnp.tanh(c * (x + 0.044715 * (x * x * x))))]


def _ep_sigmoid2(accs, extras):
    return [_sigmoid(accs[0]), _sigmoid(accs[1])]


def _head_rms(acc, gain):
    parts = []
    for h in range(acc.shape[1] // HEAD_DIM):
        y = acc[:, h * HEAD_DIM:(h + 1) * HEAD_DIM]
        ms = jnp.mean(y * y, axis=-1, keepdims=True)
        parts.append(y * lax.rsqrt(ms + NORM_EPS) * gain)
    return jnp.concatenate(parts, axis=1)


def _ep_head_rms2(accs, extras):
    return [_head_rms(accs[0], extras[0]), _head_rms(accs[1], extras[0])]


def _ep_kv(accs, extras):
    k = _head_rms(accs[0], extras[0])
    return [k, k, accs[1], accs[1]]


def _ep_gate_mul(accs, extras):
    return [extras[0] * accs[0]]


def _ep_gate_mul_add(accs, extras):
    return [extras[1] + extras[0] * accs[0]]


def _ep_residual(accs, extras):
    return [extras[0] + accs[0]]


def _down_kernel(*refs, n_i, n_j, k_half, norm_d):
    a_p, a_s, w_ref, r_p, r_s = refs[:5]
    pos = 5
    if norm_d:
        gain_ref = refs[pos]
        pos += 1
    o_p, o_s = refs[pos:pos + 2]
    if norm_d:
        xb_refs, rstd_refs, ss_refs = (tuple(refs[pos + 2 + 2 * t:pos + 4 + 2 * t])
                                       for t in range(3))
    i = pl.program_id(0)
    j = pl.program_id(1)
    kk = pl.program_id(2)
    w = w_ref[...].astype(BF16)
    k0 = pl.multiple_of(kk * k_half, V7X_LANES)

    def compute(which, a_ref, r_ref, o_ref):
        d = MACARON_WEIGHT * jnp.dot(a_ref[:, pl.ds(k0, k_half)], w,
                                     preferred_element_type=F32)

        @pl.when(kk == 0)
        def _():
            o_ref[...] = r_ref[...] + d

        @pl.when(kk != 0)
        def _():
            o = o_ref[...] + d
            o_ref[...] = o
            if norm_d:
                _emit_norm(o, gain_ref[...], xb_refs[which], rstd_refs[which],
                           ss_refs[which], j, n_j, norm_d)

    compute(0, a_p, r_p, o_p)

    @pl.when(i == n_i - 1)
    def _():
        compute(1, a_s, r_s, o_s)


def _ffn_down(a, w, res, norm_gain, *, tm, tn):
    a_p, a_s = a
    r_p, r_s = res
    t_p, k = a_p.shape
    t_s = a_s.shape[0]
    n_cols = w.shape[1]
    assert k % 2 == 0 and (k // 2) % V7X_LANES == 0
    k_half = k // 2
    n_i = t_p // tm
    n_j = n_cols // tn
    map_p, map_s = _tile_maps(n_i, n_j)
    tile_p = pl.BlockSpec((tm, tn), map_p)
    tile_s = pl.BlockSpec((t_s, tn), map_s)
    pipelined = (_nbytes((k_half, tn), F32) + 2 * _nbytes((tm + t_s, tn), F32)
                 + _nbytes((t_s, k), BF16))
    resident = (_nbytes((tm, k), BF16) + _nbytes((k_half, tn), BF16)
                + 2 * _nbytes((tm, tn), F32))
    in_specs = [
        pl.BlockSpec((tm, k), lambda i, j, kk: (i, 0), pipeline_mode=pl.Buffered(1)),
        pl.BlockSpec((t_s, k), lambda i, j, kk: (0, 0)),
        pl.BlockSpec((k_half, tn), lambda i, j, kk: (kk, j)),
        tile_p, tile_s]
    args = [a_p, a_s, w, r_p, r_s]
    out_specs = [tile_p, tile_s]
    out_shape = [jax.ShapeDtypeStruct((t_p, n_cols), F32),
                 jax.ShapeDtypeStruct((t_s, n_cols), F32)]
    scratch = []
    if norm_gain is not None:
        in_specs.append(pl.BlockSpec((1, tn), lambda i, j, kk: (0, j)))
        args.append(norm_gain)
        side = _norm_side_specs(t_p, t_s, n_cols, tm, tn, map_p, map_s)
        out_specs += side[0]
        out_shape += side[1]
        scratch = side[2]
        pipelined += side[3]
    res = pl.pallas_call(
        functools.partial(_down_kernel, n_i=n_i, n_j=n_j, k_half=k_half,
                          norm_d=n_cols if norm_gain is not None else 0),
        grid=(n_i, n_j, 2),
        in_specs=in_specs,
        out_specs=out_specs,
        out_shape=out_shape,
        scratch_shapes=scratch,
        compiler_params=_params(3, pipelined, resident),
        name="ffn_down",
    )(*args)
    return [(res[2 * t], res[2 * t + 1]) for t in range(len(res) // 2)]


def _log_sigmoid(x):
    return jnp.minimum(x, 0.0) - jnp.log1p(jnp.exp(-jnp.abs(x)))


def _rg_gate_block(xc, wa, wx, ba, bx, lam, first_pos):
    xb = xc.astype(BF16)
    r = _sigmoid(jnp.dot(xb, wa, preferred_element_type=F32) + ba)
    g = _sigmoid(jnp.dot(xb, wx, preferred_element_type=F32) + bx)
    a = jnp.exp(r * (RG_C * _log_sigmoid(lam)))
    v = 1.0 - a * a
    mult = jnp.where(v > 0.0, v * lax.rsqrt(v), 0.0)
    if first_pos is not None:
        mult = jnp.where(first_pos, 1.0, mult)
    return a, mult * g * xc


def _lru_prompt_kernel(xr_ref, gy_ref, cw_ref, cb_ref, wa_ref, wx_ref, ba_ref, bx_ref,
                       lam_ref, hg_ref, hlast_ref, x_tail, a_buf, b_buf, h_carry, *, tc):
    t = pl.program_id(0)
    d = xr_ref.shape[1]
    bw = d // N_RG_BLOCKS
    pad = V7X_SUBLANES

    @pl.when(t == 0)
    def _():
        x_tail[...] = jnp.zeros_like(x_tail)
        h_carry[...] = jnp.zeros_like(h_carry)

    row = lax.broadcasted_iota(jnp.int32, (tc, 1), 0)
    first_pos = (row + t * tc) == 0
    sub = lax.broadcasted_iota(jnp.int32, (1, V7X_SUBLANES, bw), 1)
    for c in range(N_RG_BLOCKS):
        sl = slice(c * bw, (c + 1) * bw)
        x = xr_ref[:, sl]
        x_ext = jnp.concatenate([x_tail[:, sl], x], axis=0)
        x_tail[:, sl] = x[tc - pad:, :]
        shifted = [x_ext]
        for _ in range(CONV_W - 1):
            shifted.append(pltpu.roll(shifted[-1], 1, axis=0))
        xc = cb_ref[:, sl]
        for j in range(CONV_W):
            back = CONV_W - 1 - j
            xs = x if back == 0 else shifted[back][pad:, :]
            xc = xc + cw_ref[j:j + 1, sl] * xs
        a, b = _rg_gate_block(xc, wa_ref[c], wx_ref[c], ba_ref[:, sl], bx_ref[:, sl],
                              lam_ref[:, sl], first_pos)
        a = a.reshape(tc // V7X_SUBLANES, V7X_SUBLANES, bw)
        b = b.reshape(tc // V7X_SUBLANES, V7X_SUBLANES, bw)
        for s in (1, 2, 4):
            keep = sub >= s
            a_sh = pltpu.roll(a, s, axis=1)
            b_sh = pltpu.roll(b, s, axis=1)
            b = jnp.where(keep, a * b_sh + b, b)
            a = jnp.where(keep, a * a_sh, a)
        a_buf[:, sl] = a.reshape(tc, bw)
        b_buf[:, sl] = b.reshape(tc, bw)

    def group(gi, h_prev):
        r0 = pl.multiple_of(gi * V7X_SUBLANES, V7X_SUBLANES)
        h = b_buf[pl.ds(r0, V7X_SUBLANES), :] + a_buf[pl.ds(r0, V7X_SUBLANES), :] * h_prev
        b_buf[pl.ds(r0, V7X_SUBLANES), :] = h
        return jnp.broadcast_to(h[V7X_SUBLANES - 1:V7X_SUBLANES, :], h.shape)

    h_last = lax.fori_loop(0, tc // V7X_SUBLANES, group, h_carry[...])
    h_carry[...] = h_last
    hlast_ref[...] = h_last[0:1, :]
    hg_ref[...] = (b_buf[...] * gy_ref[...]).astype(hg_ref.dtype)


def _lru_prompt(xr, gy, conv_w, conv_b, wa, wx, ba, bx, lam, *, tc):
    t_p, d = xr.shape
    bw = d // N_RG_BLOCKS
    row = pl.BlockSpec((1, d), lambda t: (0, 0))
    wspec = pl.BlockSpec((N_RG_BLOCKS, bw, bw), lambda t: (0, 0, 0))
    chunk = pl.BlockSpec((tc, d), lambda t: (t, 0))
    pipelined = (2 * _nbytes((tc, d), F32) + _nbytes((tc, d), BF16)
                 + 2 * _nbytes((N_RG_BLOCKS, bw, bw), BF16))
    scratch = [pltpu.VMEM((V7X_SUBLANES, d), F32), pltpu.VMEM((tc, d), F32),
               pltpu.VMEM((tc, d), F32), pltpu.VMEM((V7X_SUBLANES, d), F32)]
    resident = 4 * _nbytes((tc, d), F32)
    return pl.pallas_call(
        functools.partial(_lru_prompt_kernel, tc=tc),
        grid=(t_p // tc,),
        in_specs=[chunk, chunk, pl.BlockSpec((CONV_W, d), lambda t: (0, 0)), row,
                  wspec, wspec, row, row, row],
        out_specs=[chunk, row],
        out_shape=[jax.ShapeDtypeStruct((t_p, d), BF16),
                   jax.ShapeDtypeStruct((1, d), F32)],
        scratch_shapes=scratch,
        compiler_params=_params(1, pipelined, resident),
        name="lru_prompt",
    )(xr, gy, conv_w, conv_b, wa, wx, ba, bx, lam)


def _lru_sample_kernel(xr_ref, gy_ref, s0_ref, s1_ref, s2_ref, h0_ref, cw_ref, cb_ref,
                       wa_ref, wx_ref, ba_ref, bx_ref, lam_ref, hg_ref, h_ref):
    xc = cb_ref[...]
    for j, s_ref in enumerate((s0_ref, s1_ref, s2_ref)):
        xc = xc + cw_ref[j:j + 1, :] * s_ref[...]
    xc = xc + cw_ref[CONV_W - 1:CONV_W, :] * xr_ref[...]
    a, u = _rg_gate_block(xc, wa_ref[0], wx_ref[0], ba_ref[...], bx_ref[...],
                          lam_ref[...], None)
    h = u + a * h0_ref[...]
    h_ref[...] = h
    hg_ref[...] = (h * gy_ref[...]).astype(hg_ref.dtype)


def _lru_sample(xr, gy, state_conv, state_h, conv_w, conv_b, wa, wx, ba, bx, lam):
    t_s, d = xr.shape
    bw = d // N_RG_BLOCKS
    assert CONV_W == 4
    tile = pl.BlockSpec((t_s, bw), lambda c: (0, c))
    row = pl.BlockSpec((1, bw), lambda c: (0, c))
    wspec = pl.BlockSpec((1, bw, bw), lambda c: (c, 0, 0))
    state = [pl.BlockSpec((t_s, bw), lambda c, j=j: (0, j * N_RG_BLOCKS + c))
             for j in range(CONV_W - 1)]
    pipelined = 9 * _nbytes((t_s, bw), F32) + 2 * _nbytes((bw, bw), BF16)
    return pl.pallas_call(
        _lru_sample_kernel,
        grid=(N_RG_BLOCKS,),
        in_specs=[tile, tile] + state + [tile, pl.BlockSpec((CONV_W, bw), lambda c: (0, c)),
                                         row, wspec, wspec, row, row, row],
        out_specs=[tile, tile],
        out_shape=[jax.ShapeDtypeStruct((t_s, d), BF16),
                   jax.ShapeDtypeStruct((t_s, d), F32)],
        compiler_params=_params(1, pipelined, 8 * _nbytes((t_s, bw), F32)),
        name="lru_sample",
    )(xr, gy, state_conv, state_conv, state_conv, state_h, conv_w, conv_b, wa, wx,
      ba, bx, lam)


def _alibi_slope(head, n_heads):
    return 2.0 ** (-8.0 * (head + 1) / n_heads)


def _softmax_sink(s, sink):
    m = jnp.maximum(jnp.max(s, axis=-1, keepdims=True), sink)
    p = jnp.exp(s - m)
    return p, jnp.sum(p, axis=-1, keepdims=True) + jnp.exp(sink - m)


def _attn_prompt_kernel(*refs, n_heads, n_q):
    sinks_ref = refs[0]
    q_refs = refs[1:1 + n_q]
    kp_ref, kc_ref, vp_ref, vc_ref, o_ref, bias_ref = refs[1 + n_q:]
    n = pl.program_id(0)
    blk = o_ref.shape[0]
    group = n_heads // N_KV_HEADS
    heads_per_q = n_heads // n_q
    log2e = math.log2(math.e)

    @pl.when(n == 0)
    def _():
        qi = lax.broadcasted_iota(jnp.int32, (blk, 2 * blk), 0)
        kj = lax.broadcasted_iota(jnp.int32, (blk, 2 * blk), 1)
        dist = qi + blk - kj
        in_band = (dist >= 0) & (dist < WINDOW)
        dist_f = dist.astype(F32)
        for head in range(n_heads):
            term = jnp.where(in_band, (_alibi_slope(head, n_heads) * log2e) * dist_f, jnp.inf)
            bias_ref[1, head] = term
            bias_ref[0, head] = jnp.where(kj >= blk, term, jnp.inf)

    which = jnp.minimum(n, 1)
    scale = HEAD_DIM ** -0.5 * log2e
    for kv in range(N_KV_HEADS):
        cols = slice(kv * HEAD_DIM, (kv + 1) * HEAD_DIM)
        keys = jnp.concatenate([kp_ref[:, cols], kc_ref[:, cols]], axis=0)
        vals = jnp.concatenate([vp_ref[:, cols], vc_ref[:, cols]], axis=0)
        for g in range(group):
            head = kv * group + g
            q_ref = q_refs[head // heads_per_q]
            local = head % heads_per_q
            q = q_ref[:, local * HEAD_DIM:(local + 1) * HEAD_DIM]
            s = lax.dot_general(q, keys, (((1,), (1,)), ((), ())),
                                preferred_element_type=F32)
            s = s * scale - bias_ref[which, head]
            sink = jnp.full((blk, 1), sinks_ref[head] * log2e, F32)
            m = jnp.maximum(jnp.max(s, axis=-1, keepdims=True), sink)
            p = jnp.exp2(s - m)
            denom = jnp.sum(p, axis=-1, keepdims=True) + jnp.exp2(sink - m)
            o = jnp.dot(p.astype(BF16), vals, preferred_element_type=F32)
            o_ref[:, head * HEAD_DIM:(head + 1) * HEAD_DIM] = (o / denom).astype(o_ref.dtype)


def _attn_prompt(qs, k, v, sinks):
    t_p = k.shape[0]
    dq = sum(q.shape[1] for q in qs)
    dk = k.shape[1]
    blk = WINDOW
    n_heads = dq // HEAD_DIM
    cur = lambda n: (n, 0)
    prev = lambda n: (jnp.maximum(n - 1, 0), 0)
    pipelined = 2 * _nbytes((blk, dq), BF16) + 4 * _nbytes((blk, dk), BF16)
    bias_shape = (2, n_heads, blk, 2 * blk)
    return pl.pallas_call(
        functools.partial(_attn_prompt_kernel, n_heads=n_heads, n_q=len(qs)),
        grid=(t_p // blk,),
        in_specs=[pl.BlockSpec(memory_space=pltpu.SMEM)]
        + [pl.BlockSpec((blk, q.shape[1]), cur) for q in qs]
        + [pl.BlockSpec((blk, dk), prev), pl.BlockSpec((blk, dk), cur),
           pl.BlockSpec((blk, dk), prev), pl.BlockSpec((blk, dk), cur)],
        out_specs=pl.BlockSpec((blk, dq), cur),
        out_shape=jax.ShapeDtypeStruct((t_p, dq), BF16),
        scratch_shapes=[pltpu.VMEM(bias_shape, F32)],
        compiler_params=_params(1, pipelined, _nbytes(bias_shape, F32)
                                + 16 * _nbytes((blk, 2 * blk), F32)),
        name="attn_prompt",
    )(sinks, *qs, k, k, v, v)


def _attn_sample_kernel(sink_ref, q_ref, kn_ref, vn_ref, ck_ref, cv_ref,
                        o_ref, ok_ref, ov_ref, *, n_heads, bb):
    group = n_heads // N_KV_HEADS
    rows = WINDOW * N_KV_HEADS
    col = lax.broadcasted_iota(jnp.int32, (n_heads, rows), 1)
    head = lax.broadcasted_iota(jnp.int32, (n_heads, rows), 0)
    own_kv = jnp.bitwise_and(col, N_KV_HEADS - 1) == _shift_div(head, group)
    dist = (WINDOW - 1 - _shift_div(col, N_KV_HEADS)).astype(F32)
    slope = jnp.exp2(-8.0 * (head + 1).astype(F32) / n_heads)
    bias = jnp.where(own_kv, slope * dist, jnp.inf)
    sink = sink_ref[...]
    scale = HEAD_DIM ** -0.5

    for b in range(bb):
        for c_ref, n_ref, out_ref in ((ck_ref, kn_ref, ok_ref), (cv_ref, vn_ref, ov_ref)):
            out_ref[b, 0:rows - N_KV_HEADS, :] = c_ref[b, N_KV_HEADS:rows, :]
            for kv in range(N_KV_HEADS):
                r = rows - N_KV_HEADS + kv
                out_ref[b, r:r + 1, :] = n_ref[b:b + 1, kv * HEAD_DIM:(kv + 1) * HEAD_DIM]
        keys = ok_ref[b].astype(BF16)
        vals = ov_ref[b].astype(BF16)
        s = lax.dot_general(q_ref[b], keys, (((1,), (1,)), ((), ())),
                            preferred_element_type=F32)
        p, denom = _softmax_sink(s * scale - bias, sink)
        o = jnp.dot(p.astype(BF16), vals, preferred_element_type=F32)
        o_ref[b] = (o / denom).astype(o_ref.dtype)


def _attn_sample(q, k_new, v_new, cache_k, cache_v, sink_col, *, bb):
    t_s, n_heads, _ = q.shape
    dk = N_KV_HEADS * HEAD_DIM
    rows = WINDOW * N_KV_HEADS
    cache = pl.BlockSpec((bb, rows, HEAD_DIM), lambda i: (i, 0, 0))
    new = pl.BlockSpec((bb, dk), lambda i: (i, 0))
    qspec = pl.BlockSpec((bb, n_heads, HEAD_DIM), lambda i: (i, 0, 0))
    pipelined = (4 * _nbytes((bb, rows, HEAD_DIM), F32) + 2 * _nbytes((bb, dk), F32)
                 + 2 * _nbytes((bb, n_heads, HEAD_DIM), BF16))
    return pl.pallas_call(
        functools.partial(_attn_sample_kernel, n_heads=n_heads, bb=bb),
        grid=(t_s // bb,),
        in_specs=[pl.BlockSpec((n_heads, 1), lambda i: (0, 0)), qspec, new, new,
                  cache, cache],
        out_specs=[qspec, cache, cache],
        out_shape=[jax.ShapeDtypeStruct((t_s, n_heads, HEAD_DIM), BF16),
                   jax.ShapeDtypeStruct((t_s, rows, HEAD_DIM), F32),
                   jax.ShapeDtypeStruct((t_s, rows, HEAD_DIM), F32)],
        compiler_params=_params(1, pipelined, 8 * _nbytes((WINDOW, dk), F32)),
        name="attn_sample",
    )(sink_col, q, k_new, v_new, cache_k, cache_v)


def _swiglu_ffn(x, h, rstd, w_gate, w_up, w_down, next_gain, *, tm, tm_down, tn):
    d_ff = w_gate.shape[1]
    (a,) = _matmul("ffn_gate_up", [h], [(w_gate, 0), (w_up, 0)], [], [BF16], _ep_swiglu,
                   n_cols=d_ff, tm=tm, tn=tn, rowscale=rstd)
    return _ffn_down(a, w_down, x, next_gain, tm=tm_down, tn=tn)


def kernel(x_prompt, x_sample, state_conv, state_h, cache_k, cache_v, norm_ffn1, w_ffn1_gate, w_ffn1_up, w_ffn1_down, norm_mix, w_in, conv_w, conv_b, rg_w_a, rg_b_a, rg_w_x, rg_b_x, rg_lambda, q_norm, k_norm, sinks, w_lru_proj, w_attn_proj, w_out, norm_ffn2, w_ffn2_gate, w_ffn2_up, w_ffn2_down):
    b_p, t_p, d = x_prompt.shape
    t_s = x_sample.shape[0]
    depth = w_in.shape[0]
    assert b_p == 1 and x_sample.shape[1] == 1 and depth == 1
    n_heads = d // HEAD_DIM
    dk = N_KV_HEADS * HEAD_DIM
    tm, tm_down, tn = 2048, 1024, V7X_MXU_DIM
    mm = functools.partial(_matmul, tm=tm, tn=tn)
    ffn = functools.partial(_swiglu_ffn, tm=tm, tm_down=tm_down, tn=tn)

    x = (x_prompt.reshape(t_p, d), x_sample.reshape(t_s, d))
    h = tuple(_rmsnorm(xi, norm_ffn1[0]) for xi in x)
    x, xb, rstd = ffn(x, h, None, w_ffn1_gate[0], w_ffn1_up[0], w_ffn1_down[0],
                      norm_mix[0].reshape(1, d))

    w = w_in[0]
    xr, gy = mm("in_rec", [xb], [(w, 0), (w, d)], [], [F32, F32], _ep_xr_gelu, n_cols=d,
                rowscale=rstd)
    q_lo, q_hi = mm("in_q", [xb], [(w, 2 * d), (w, 2 * d + d // 2)],
                    [("const", q_norm[0].reshape(1, HEAD_DIM))], [BF16, BF16],
                    _ep_head_rms2, n_cols=d // 2, rowscale=rstd)
    k32, kb, v32, vb = mm("in_kv", [xb], [(w, 3 * d), (w, 3 * d + dk)],
                          [("const", k_norm[0].reshape(1, HEAD_DIM))],
                          [F32, BF16, F32, BF16], _ep_kv, n_cols=dk, rowscale=rstd)
    sg_rec, sg_att = mm("in_gates", [xb], [(w, 3 * d + 2 * dk), (w, 4 * d + 2 * dk)], [],
                        [F32, F32], _ep_sigmoid2, n_cols=d, rowscale=rstd)

    cw, cb = conv_w[0], conv_b[0].reshape(1, d)
    wa, wx = rg_w_a[0].astype(BF16), rg_w_x[0].astype(BF16)
    ba, bx = rg_b_a[0].reshape(1, d), rg_b_x[0].reshape(1, d)
    lam = rg_lambda[0].reshape(1, d)
    hg_p, h_last = _lru_prompt(xr[0], gy[0], cw, cb, wa, wx, ba, bx, lam, tc=256)
    hg_s, h_s = _lru_sample(xr[1], gy[1], state_conv[0].reshape(t_s, (CONV_W - 1) * d),
                            state_h[0], cw, cb, wa, wx, ba, bx, lam)

    o_p = _attn_prompt([q_lo[0], q_hi[0]], kb[0], vb[0], sinks[0])
    q_s = jnp.concatenate([q_lo[1], q_hi[1]], axis=1)
    o_s, new_k, new_v = _attn_sample(
        q_s.reshape(t_s, n_heads, HEAD_DIM), k32[1], v32[1],
        cache_k[0].reshape(t_s, WINDOW * N_KV_HEADS, HEAD_DIM),
        cache_v[0].reshape(t_s, WINDOW * N_KV_HEADS, HEAD_DIM),
        sinks[0].reshape(n_heads, 1), bb=8)
    o = (o_p, o_s.reshape(t_s, d))

    mm1 = functools.partial(_matmul, tm=tm // 2, tn=2 * tn, n_cols=d)
    (y_rec,) = mm1("lru_proj", [(hg_p, hg_s)], [(w_lru_proj[0], 0)], [("pair",) + sg_rec],
                   [F32], _ep_gate_mul)
    (mixed,) = mm1("attn_proj", [o], [(w_attn_proj[0], 0)],
                   [("pair",) + sg_att, ("pair",) + y_rec], [BF16], _ep_gate_mul_add)
    x, xb, rstd = mm1("out_proj", [mixed], [(w_out[0], 0)], [("pair",) + x], [F32],
                      _ep_residual, norm_gain=norm_ffn2[0].reshape(1, d))

    (y,) = ffn(x, xb, rstd, w_ffn2_gate[0], w_ffn2_up[0], w_ffn2_down[0], None)

    kv_shape = (1, -1, WINDOW, N_KV_HEADS, HEAD_DIM)
    n_conv = CONV_W - 1
    sample_conv = jnp.concatenate([state_conv[0][:, 1:], xr[1][:, None, :]], axis=1)
    return (y[0].reshape(1, t_p, d), y[1].reshape(t_s, 1, d),
            xr[0][t_p - n_conv:].reshape(1, 1, n_conv, d), h_last.reshape(1, 1, d),
            k32[0][t_p - WINDOW:].reshape(kv_shape), v32[0][t_p - WINDOW:].reshape(kv_shape),
            sample_conv[None], h_s[None],
            new_k.reshape(kv_shape), new_v.reshape(kv_shape))
```

```python
import functools
import math

import jax
import jax.numpy as jnp
from jax import lax
from jax.experimental import pallas as pl
from jax.experimental.pallas import tpu as pltpu

HEAD_DIM = 128
N_KV_HEADS = 8
WINDOW = 128
N_RG_BLOCKS = 16
CONV_W = 4
RG_C = 8.0
NORM_EPS = 1e-6
MACARON_WEIGHT = 0.5

V7X_VMEM_BYTES = 64 * 1024 * 1024
V7X_LANES = 128
V7X_SUBLANES = 8
V7X_MXU_DIM = 256
V7X_VMEM_RESERVE_BYTES = 4 * 1024 * 1024
V7X_VMEM_MAX_REQUEST_BYTES = V7X_VMEM_BYTES - 1024 * 1024

F32 = jnp.float32
BF16 = jnp.bfloat16


def _nbytes(shape, dtype):
    return math.prod(shape) * jnp.dtype(dtype).itemsize


def _vmem_limit(pipelined_bytes, resident_bytes):
    want = 2 * pipelined_bytes + resident_bytes + V7X_VMEM_RESERVE_BYTES
    return int(min(want, V7X_VMEM_MAX_REQUEST_BYTES))


def _params(n_grid, pipelined_bytes, resident_bytes):
    return pltpu.CompilerParams(
        dimension_semantics=("arbitrary",) * n_grid,
        vmem_limit_bytes=_vmem_limit(pipelined_bytes, resident_bytes))


def _shift_div(x, n):
    assert n & (n - 1) == 0
    return lax.shift_right_logical(x, jnp.int32(n.bit_length() - 1))


def _rms_kernel(x_ref, g_ref, o_ref):
    x = x_ref[...]
    ms = jnp.mean(x * x, axis=-1, keepdims=True)
    o_ref[...] = (x * lax.rsqrt(ms + NORM_EPS) * g_ref[...]).astype(o_ref.dtype)


def _rmsnorm(x, g):
    rows, d = x.shape
    rb = min(rows, 256)
    assert rows % rb == 0
    blk = _nbytes((rb, d), F32) + _nbytes((rb, d), BF16)
    return pl.pallas_call(
        _rms_kernel,
        grid=(rows // rb,),
        in_specs=[pl.BlockSpec((rb, d), lambda i: (i, 0)),
                  pl.BlockSpec((1, d), lambda i: (0, 0))],
        out_specs=pl.BlockSpec((rb, d), lambda i: (i, 0)),
        out_shape=jax.ShapeDtypeStruct((rows, d), BF16),
        compiler_params=_params(1, blk, 3 * _nbytes((rb, d), F32)),
        name="rmsnorm",
    )(x, g.reshape(1, d))


def _tile_maps(n_i, n_j):
    def prompt(i, j, *_):
        return (i, j)

    def sample(i, j, *_):
        return (0, jnp.where(i == n_i - 1, j, 0))

    return prompt, sample


def _emit_norm(x, gain, xb_ref, rstd_ref, ss_ref, j, n_j, d):
    xb_ref[...] = (x * gain).astype(xb_ref.dtype)
    sq = x * x
    part = sq[:, 0:V7X_LANES]
    for t in range(1, x.shape[1] // V7X_LANES):
        part = part + sq[:, t * V7X_LANES:(t + 1) * V7X_LANES]

    @pl.when(j == 0)
    def _():
        ss_ref[...] = part

    @pl.when(j != 0)
    def _():
        ss_ref[...] += part

    @pl.when(j == n_j - 1)
    def _():
        ms = jnp.sum(ss_ref[...], axis=-1, keepdims=True) / d
        rstd_ref[...] = jnp.broadcast_to(lax.rsqrt(ms + NORM_EPS), rstd_ref.shape)


def _mm_kernel(*refs, n_x, n_w, extra_kinds, n_out, epilogue, n_i, n_j, rowscale, norm_d):
    pos = 0
    xs = [(refs[pos + 2 * t], refs[pos + 2 * t + 1]) for t in range(n_x)]
    pos += 2 * n_x
    ws = list(refs[pos:pos + n_w])
    pos += n_w
    extras = []
    for kind in extra_kinds:
        if kind == "pair":
            extras.append((refs[pos], refs[pos + 1]))
            pos += 2
        else:
            extras.append((refs[pos], refs[pos]))
            pos += 1
    if rowscale:
        scale_refs = (refs[pos], refs[pos + 1])
        pos += 2
    if norm_d:
        gain_ref = refs[pos]
        pos += 1
    outs = [(refs[pos + 2 * t], refs[pos + 2 * t + 1]) for t in range(n_out)]
    pos += 2 * n_out
    if norm_d:
        xb_refs, rstd_refs, ss_refs = (tuple(refs[pos + 2 * t:pos + 2 * t + 2])
                                       for t in range(3))
    i = pl.program_id(0)
    j = pl.program_id(1)
    wb = [w[...].astype(BF16) for w in ws]

    def compute(which):
        accs = []
        for t in range(n_w):
            x = xs[t if n_x > 1 else 0][which][...]
            accs.append(jnp.dot(x, wb[t], preferred_element_type=F32))
        if rowscale:
            r = scale_refs[which][...]
            scale = jnp.concatenate([r] * (accs[0].shape[1] // V7X_LANES), axis=1)
            accs = [acc * scale for acc in accs]
        res = epilogue(accs, [e[which][...] for e in extras])
        for o, r in zip(outs, res):
            o[which][...] = r.astype(o[which].dtype)
        if norm_d:
            _emit_norm(res[0], gain_ref[...], xb_refs[which], rstd_refs[which],
                       ss_refs[which], j, n_j, norm_d)

    compute(0)

    @pl.when(i == n_i - 1)
    def _():
        compute(1)


def _norm_side_specs(t_p, t_s, n_cols, tm, tn, map_p, map_s):
    rows_p = pl.BlockSpec((tm, V7X_LANES), lambda i, *_: (i, 0))
    rows_s = pl.BlockSpec((t_s, V7X_LANES), lambda i, *_: (0, 0))
    specs = [pl.BlockSpec((tm, tn), map_p), pl.BlockSpec((t_s, tn), map_s), rows_p, rows_s]
    shapes = [jax.ShapeDtypeStruct((t_p, n_cols), BF16),
              jax.ShapeDtypeStruct((t_s, n_cols), BF16),
              jax.ShapeDtypeStruct((t_p, V7X_LANES), F32),
              jax.ShapeDtypeStruct((t_s, V7X_LANES), F32)]
    scratch = [pltpu.VMEM((tm, V7X_LANES), F32), pltpu.VMEM((t_s, V7X_LANES), F32)]
    nbytes = _nbytes((tm + t_s, tn), BF16) + 2 * _nbytes((tm + t_s, V7X_LANES), F32)
    return specs, shapes, scratch, nbytes


def _matmul(name, xs, ws, extras, out_dtypes, epilogue, *, n_cols, tm, tn, rowscale=None,
            norm_gain=None):
    t_p = xs[0][0].shape[0]
    t_s = xs[0][1].shape[0]
    x_widths = [x_p.shape[1] for x_p, _ in xs]
    k = x_widths[0]
    assert len(xs) in (1, len(ws)) and all(kx == k for kx in x_widths)
    assert t_p % tm == 0 and n_cols % tn == 0
    n_i = t_p // tm
    n_j = n_cols // tn
    map_p, map_s = _tile_maps(n_i, n_j)

    in_specs, args = [], []
    pipelined = 0
    for w, col0 in ws:
        assert col0 % tn == 0 and w.shape[0] == k
        in_specs.append(pl.BlockSpec((k, tn), lambda i, j, c=col0 // tn: (0, j + c)))
        args.append(w)
        pipelined += _nbytes((k, tn), w.dtype)
    kinds = []
    for e in extras:
        kinds.append(e[0])
        if e[0] == "pair":
            in_specs += [pl.BlockSpec((tm, tn), map_p), pl.BlockSpec((t_s, tn), map_s)]
            args += [e[1], e[2]]
            pipelined += _nbytes((tm + t_s, tn), e[1].dtype)
        elif e[0] == "row":
            in_specs.append(pl.BlockSpec((1, tn), lambda i, j: (0, j)))
            args.append(e[1])
        else:
            assert e[0] == "const" and e[1].ndim == 2
            in_specs.append(pl.BlockSpec(e[1].shape, lambda i, j: (0, 0)))
            args.append(e[1])
    if rowscale is not None:
        in_specs += [pl.BlockSpec((tm, V7X_LANES), lambda i, j: (i, 0)),
                     pl.BlockSpec((t_s, V7X_LANES), lambda i, j: (0, 0))]
        args += list(rowscale)
        pipelined += _nbytes((tm + t_s, V7X_LANES), F32)
    if norm_gain is not None:
        in_specs.append(pl.BlockSpec((1, tn), lambda i, j: (0, j)))
        args.append(norm_gain)
    out_specs, out_shape, scratch = [], [], []
    for dt in out_dtypes:
        out_specs += [pl.BlockSpec((tm, tn), map_p), pl.BlockSpec((t_s, tn), map_s)]
        out_shape += [jax.ShapeDtypeStruct((t_p, n_cols), dt),
                      jax.ShapeDtypeStruct((t_s, n_cols), dt)]
        pipelined += _nbytes((tm + t_s, tn), dt)
    if norm_gain is not None:
        assert n_cols == norm_gain.shape[1]
        side = _norm_side_specs(t_p, t_s, n_cols, tm, tn, map_p, map_s)
        out_specs += side[0]
        out_shape += side[1]
        scratch = side[2]
        pipelined += side[3]
    resident = len(ws) * (_nbytes((k, tn), BF16) + 4 * _nbytes((tm, tn), F32))
    pipelined += _nbytes((t_s, sum(x_widths)), BF16)
    x_bytes = _nbytes((tm, sum(x_widths)), BF16)
    double_x = (2 * (pipelined + x_bytes) + resident + V7X_VMEM_RESERVE_BYTES
                <= V7X_VMEM_MAX_REQUEST_BYTES)
    if double_x:
        pipelined += x_bytes
    else:
        resident += x_bytes
    x_specs, x_args = [], []
    for (x_p, x_s), kx in zip(xs, x_widths):
        assert x_p.dtype == BF16 and x_s.dtype == BF16
        if double_x:
            x_spec = pl.BlockSpec((tm, kx), lambda i, j: (i, 0))
        else:
            x_spec = pl.BlockSpec((tm, kx), lambda i, j: (i, 0),
                                  pipeline_mode=pl.Buffered(1))
        x_specs += [x_spec, pl.BlockSpec((t_s, kx), lambda i, j: (0, 0))]
        x_args += [x_p, x_s]
    in_specs = x_specs + in_specs
    args = x_args + args
    res = pl.pallas_call(
        functools.partial(_mm_kernel, n_x=len(xs), n_w=len(ws), extra_kinds=tuple(kinds),
                          n_out=len(out_dtypes), epilogue=epilogue, n_i=n_i, n_j=n_j,
                          rowscale=rowscale is not None,
                          norm_d=n_cols if norm_gain is not None else 0),
        grid=(n_i, n_j),
        in_specs=in_specs,
        out_specs=out_specs,
        out_shape=out_shape,
        scratch_shapes=scratch,
        compiler_params=_params(2, pipelined, resident),
        name=name,
    )(*args)
    return [(res[2 * t], res[2 * t + 1]) for t in range(len(res) // 2)]


def _sigmoid(x):
    return 1.0 / (1.0 + jnp.exp(-x))


def _ep_swiglu(accs, extras):
    g, u = accs
    return [g * _sigmoid(g) * u]


def _ep_xr_gelu(accs, extras):
    x = accs[1]
    c = math.sqrt(2.0 / math.pi)
    return [accs[0], 0.5 * x * (1.0 + jnp.tanh(c * (x + 0.044715 * (x * x * x))))]


def _ep_sigmoid2(accs, extras):
    return [_sigmoid(accs[0]), _sigmoid(accs[1])]


def _head_rms(acc, gain):
    parts = []
    for h in range(acc.shape[1] // HEAD_DIM):
        y = acc[:, h * HEAD_DIM:(h + 1) * HEAD_DIM]
        ms = jnp.mean(y * y, axis=-1, keepdims=True)
        parts.append(y * lax.rsqrt(ms + NORM_EPS) * gain)
    return jnp.concatenate(parts, axis=1)


def _ep_head_rms2(accs, extras):
    return [_head_rms(accs[0], extras[0]), _head_rms(accs[1], extras[0])]


def _ep_kv(accs, extras):
    k = _head_rms(accs[0], extras[0])
    return [k, k, accs[1], accs[1]]


def _ep_gate_mul(accs, extras):
    return [extras[0] * accs[0]]


def _ep_gate_mul_add(accs, extras):
    return [extras[1] + extras[0] * accs[0]]


def _ep_residual(accs, extras):
    return [extras[0] + accs[0]]


def _down_kernel(*refs, n_i, n_j, k_half, norm_d):
    a_p, a_s, w_ref, r_p, r_s = refs[:5]
    pos = 5
    if norm_d:
        gain_ref = refs[pos]
        pos += 1
    o_p, o_s = refs[pos:pos + 2]
    if norm_d:
        xb_refs, rstd_refs, ss_refs = (tuple(refs[pos + 2 + 2 * t:pos + 4 + 2 * t])
                                       for t in range(3))
    i = pl.program_id(0)
    j = pl.program_id(1)
    kk = pl.program_id(2)
    w = w_ref[...].astype(BF16)
    k0 = pl.multiple_of(kk * k_half, V7X_LANES)

    def compute(which, a_ref, r_ref, o_ref):
        d = MACARON_WEIGHT * jnp.dot(a_ref[:, pl.ds(k0, k_half)], w,
                                     preferred_element_type=F32)

        @pl.when(kk == 0)
        def _():
            o_ref[...] = r_ref[...] + d

        @pl.when(kk != 0)
        def _():
            o = o_ref[...] + d
            o_ref[...] = o
            if norm_d:
                _emit_norm(o, gain_ref[...], xb_refs[which], rstd_refs[which],
                           ss_refs[which], j, n_j, norm_d)

    compute(0, a_p, r_p, o_p)

    @pl.when(i == n_i - 1)
    def _():
        compute(1, a_s, r_s, o_s)


def _ffn_down(a, w, res, norm_gain, *, tm, tn):
    a_p, a_s = a
    r_p, r_s = res
    t_p, k = a_p.shape
    t_s = a_s.shape[0]
    n_cols = w.shape[1]
    assert k % 2 == 0 and (k // 2) % V7X_LANES == 0
    k_half = k // 2
    n_i = t_p // tm
    n_j = n_cols // tn
    map_p, map_s = _tile_maps(n_i, n_j)
    tile_p = pl.BlockSpec((tm, tn), map_p)
    tile_s = pl.BlockSpec((t_s, tn), map_s)
    pipelined = (_nbytes((k_half, tn), F32) + 2 * _nbytes((tm + t_s, tn), F32)
                 + _nbytes((t_s, k), BF16))
    resident = (_nbytes((tm, k), BF16) + _nbytes((k_half, tn), BF16)
                + 2 * _nbytes((tm, tn), F32))
    in_specs = [
        pl.BlockSpec((tm, k), lambda i, j, kk: (i, 0), pipeline_mode=pl.Buffered(1)),
        pl.BlockSpec((t_s, k), lambda i, j, kk: (0, 0)),
        pl.BlockSpec((k_half, tn), lambda i, j, kk: (kk, j)),
        tile_p, tile_s]
    args = [a_p, a_s, w, r_p, r_s]
    out_specs = [tile_p, tile_s]
    out_shape = [jax.ShapeDtypeStruct((t_p, n_cols), F32),
                 jax.ShapeDtypeStruct((t_s, n_cols), F32)]
    scratch = []
    if norm_gain is not None:
        in_specs.append(pl.BlockSpec((1, tn), lambda i, j, kk: (0, j)))
        args.append(norm_gain)
        side = _norm_side_specs(t_p, t_s, n_cols, tm, tn, map_p, map_s)
        out_specs += side[0]
        out_shape += side[1]
        scratch = side[2]
        pipelined += side[3]
    res = pl.pallas_call(
        functools.partial(_down_kernel, n_i=n_i, n_j=n_j, k_half=k_half,
                          norm_d=n_cols if norm_gain is not None else 0),
        grid=(n_i, n_j, 2),
        in_specs=in_specs,
        out_specs=out_specs,
        out_shape=out_shape,
        scratch_shapes=scratch,
        compiler_params=_params(3, pipelined, resident),
        name="ffn_down",
    )(*args)
    return [(res[2 * t], res[2 * t + 1]) for t in range(len(res) // 2)]


def _log_sigmoid(x):
    return jnp.minimum(x, 0.0) - jnp.log1p(jnp.exp(-jnp.abs(x)))


def _rg_gate_block(xc, wa, wx, ba, bx, lam, first_pos):
    xb = xc.astype(BF16)
    r = _sigmoid(jnp.dot(xb, wa, preferred_element_type=F32) + ba)
    g = _sigmoid(jnp.dot(xb, wx, preferred_element_type=F32) + bx)
    a = jnp.exp(r * (RG_C * _log_sigmoid(lam)))
    v = 1.0 - a * a
    mult = jnp.where(v > 0.0, v * lax.rsqrt(v), 0.0)
    if first_pos is not None:
        mult = jnp.where(first_pos, 1.0, mult)
    return a, mult * g * xc


def _lru_prompt_kernel(xr_ref, gy_ref, cw_ref, cb_ref, wa_ref, wx_ref, ba_ref, bx_ref,
                       lam_ref, hg_ref, hlast_ref, x_tail, a_buf, b_buf, h_carry, *, tc):
    t = pl.program_id(0)
    d = xr_ref.shape[1]
    bw = d // N_RG_BLOCKS
    pad = V7X_SUBLANES

    @pl.when(t == 0)
    def _():
        x_tail[...] = jnp.zeros_like(x_tail)
        h_carry[...] = jnp.zeros_like(h_carry)

    row = lax.broadcasted_iota(jnp.int32, (tc, 1), 0)
    first_pos = (row + t * tc) == 0
    n_groups = tc // V7X_SUBLANES
    slabs_per_block = bw // V7X_LANES
    for c in range(N_RG_BLOCKS):
        sl = slice(c * bw, (c + 1) * bw)
        x = xr_ref[:, sl]
        x_ext = jnp.concatenate([x_tail[:, sl], x], axis=0)
        x_tail[:, sl] = x[tc - pad:, :]
        shifted = [x_ext]
        for _ in range(CONV_W - 1):
            shifted.append(pltpu.roll(shifted[-1], 1, axis=0))
        xc = cb_ref[:, sl]
        for j in range(CONV_W):
            back = CONV_W - 1 - j
            xs = x if back == 0 else shifted[back][pad:, :]
            xc = xc + cw_ref[j:j + 1, sl] * xs
        a, b = _rg_gate_block(xc, wa_ref[c], wx_ref[c], ba_ref[:, sl], bx_ref[:, sl],
                              lam_ref[:, sl], first_pos)
        for half in range(slabs_per_block):
            s = c * slabs_per_block + half
            lanes = slice(half * V7X_LANES, (half + 1) * V7X_LANES)
            a_buf[s] = a[:, lanes]
            b_buf[s] = b[:, lanes]
            a_run = a_buf[s, pl.ds(0, n_groups, stride=V7X_SUBLANES), :]
            b_run = b_buf[s, pl.ds(0, n_groups, stride=V7X_SUBLANES), :]
            for r in range(1, V7X_SUBLANES):
                rows_r = pl.ds(r, n_groups, stride=V7X_SUBLANES)
                a_r = a_buf[s, rows_r, :]
                b_run = a_r * b_run + b_buf[s, rows_r, :]
                a_run = a_r * a_run
                a_buf[s, rows_r, :] = a_run
                b_buf[s, rows_r, :] = b_run

    def group(gi, h_prev):
        rows = pl.ds(pl.multiple_of(gi * V7X_SUBLANES, V7X_SUBLANES), V7X_SUBLANES)
        h = b_buf[:, rows, :] + a_buf[:, rows, :] * h_prev
        b_buf[:, rows, :] = h
        return jnp.broadcast_to(h[:, V7X_SUBLANES - 1:V7X_SUBLANES, :], h.shape)

    h_last = lax.fori_loop(0, n_groups, group, h_carry[...], unroll=2)
    h_carry[...] = h_last
    for s in range(d // V7X_LANES):
        lanes = slice(s * V7X_LANES, (s + 1) * V7X_LANES)
        hlast_ref[:, lanes] = h_last[s, 0:1, :]
        hg_ref[:, lanes] = (b_buf[s] * gy_ref[:, lanes]).astype(hg_ref.dtype)


def _lru_prompt(xr, gy, conv_w, conv_b, wa, wx, ba, bx, lam, *, tc):
    t_p, d = xr.shape
    bw = d // N_RG_BLOCKS
    row = pl.BlockSpec((1, d), lambda t: (0, 0))
    wspec = pl.BlockSpec((N_RG_BLOCKS, bw, bw), lambda t: (0, 0, 0))
    chunk = pl.BlockSpec((tc, d), lambda t: (t, 0))
    pipelined = (2 * _nbytes((tc, d), F32) + _nbytes((tc, d), BF16)
                 + 2 * _nbytes((N_RG_BLOCKS, bw, bw), BF16))
    n_slabs = d // V7X_LANES
    scratch = [pltpu.VMEM((V7X_SUBLANES, d), F32), pltpu.VMEM((n_slabs, tc, V7X_LANES), F32),
               pltpu.VMEM((n_slabs, tc, V7X_LANES), F32),
               pltpu.VMEM((n_slabs, V7X_SUBLANES, V7X_LANES), F32)]
    resident = 4 * _nbytes((tc, d), F32)
    return pl.pallas_call(
        functools.partial(_lru_prompt_kernel, tc=tc),
        grid=(t_p // tc,),
        in_specs=[chunk, chunk, pl.BlockSpec((CONV_W, d), lambda t: (0, 0)), row,
                  wspec, wspec, row, row, row],
        out_specs=[chunk, row],
        out_shape=[jax.ShapeDtypeStruct((t_p, d), BF16),
                   jax.ShapeDtypeStruct((1, d), F32)],
        scratch_shapes=scratch,
        compiler_params=_params(1, pipelined, resident),
        name="lru_prompt",
    )(xr, gy, conv_w, conv_b, wa, wx, ba, bx, lam)


def _lru_sample_kernel(xr_ref, gy_ref, s0_ref, s1_ref, s2_ref, h0_ref, cw_ref, cb_ref,
                       wa_ref, wx_ref, ba_ref, bx_ref, lam_ref, hg_ref, h_ref):
    xc = cb_ref[...]
    for j, s_ref in enumerate((s0_ref, s1_ref, s2_ref)):
        xc = xc + cw_ref[j:j + 1, :] * s_ref[...]
    xc = xc + cw_ref[CONV_W - 1:CONV_W, :] * xr_ref[...]
    a, u = _rg_gate_block(xc, wa_ref[0], wx_ref[0], ba_ref[...], bx_ref[...],
                          lam_ref[...], None)
    h = u + a * h0_ref[...]
    h_ref[...] = h
    hg_ref[...] = (h * gy_ref[...]).astype(hg_ref.dtype)


def _lru_sample(xr, gy, state_conv, state_h, conv_w, conv_b, wa, wx, ba, bx, lam):
    t_s, d = xr.shape
    bw = d // N_RG_BLOCKS
    assert CONV_W == 4
    tile = pl.BlockSpec((t_s, bw), lambda c: (0, c))
    row = pl.BlockSpec((1, bw), lambda c: (0, c))
    wspec = pl.BlockSpec((1, bw, bw), lambda c: (c, 0, 0))
    state = [pl.BlockSpec((t_s, bw), lambda c, j=j: (0, j * N_RG_BLOCKS + c))
             for j in range(CONV_W - 1)]
    pipelined = 9 * _nbytes((t_s, bw), F32) + 2 * _nbytes((bw, bw), BF16)
    return pl.pallas_call(
        _lru_sample_kernel,
        grid=(N_RG_BLOCKS,),
        in_specs=[tile, tile] + state + [tile, pl.BlockSpec((CONV_W, bw), lambda c: (0, c)),
                                         row, wspec, wspec, row, row, row],
        out_specs=[tile, tile],
        out_shape=[jax.ShapeDtypeStruct((t_s, d), BF16),
                   jax.ShapeDtypeStruct((t_s, d), F32)],
        compiler_params=_params(1, pipelined, 8 * _nbytes((t_s, bw), F32)),
        name="lru_sample",
    )(xr, gy, state_conv, state_conv, state_conv, state_h, conv_w, conv_b, wa, wx,
      ba, bx, lam)


def _alibi_slope(head, n_heads):
    return 2.0 ** (-8.0 * (head + 1) / n_heads)


def _softmax_sink(s, sink):
    m = jnp.maximum(jnp.max(s, axis=-1, keepdims=True), sink)
    p = jnp.exp(s - m)
    return p, jnp.sum(p, axis=-1, keepdims=True) + jnp.exp(sink - m)


def _attn_prompt_kernel(*refs, n_heads, n_q):
    sinks_ref = refs[0]
    q_refs = refs[1:1 + n_q]
    kp_ref, kc_ref, vp_ref, vc_ref, o_ref, bias_ref = refs[1 + n_q:]
    n = pl.program_id(0)
    blk = o_ref.shape[0]
    group = n_heads // N_KV_HEADS
    heads_per_q = n_heads // n_q
    log2e = math.log2(math.e)

    @pl.when(n == 0)
    def _():
        qi = lax.broadcasted_iota(jnp.int32, (blk, 2 * blk), 0)
        kj = lax.broadcasted_iota(jnp.int32, (blk, 2 * blk), 1)
        dist = qi + blk - kj
        in_band = (dist >= 0) & (dist < WINDOW)
        dist_f = dist.astype(F32)
        for head in range(n_heads):
            term = jnp.where(in_band, (_alibi_slope(head, n_heads) * log2e) * dist_f, jnp.inf)
            bias_ref[1, head] = term
            bias_ref[0, head] = jnp.where(kj >= blk, term, jnp.inf)

    which = jnp.minimum(n, 1)
    scale = HEAD_DIM ** -0.5 * log2e
    for kv in range(N_KV_HEADS):
        cols = slice(kv * HEAD_DIM, (kv + 1) * HEAD_DIM)
        keys = jnp.concatenate([kp_ref[:, cols], kc_ref[:, cols]], axis=0)
        vals = jnp.concatenate([vp_ref[:, cols], vc_ref[:, cols]], axis=0)
        for g in range(group):
            head = kv * group + g
            q_ref = q_refs[head // heads_per_q]
            local = head % heads_per_q
            q = q_ref[:, local * HEAD_DIM:(local + 1) * HEAD_DIM]
            s = lax.dot_general(q, keys, (((1,), (1,)), ((), ())),
                                preferred_element_type=F32)
            s = s * scale - bias_ref[which, head]
            sink = jnp.full((blk, 1), sinks_ref[head] * log2e, F32)
            m = jnp.maximum(jnp.max(s, axis=-1, keepdims=True), sink)
            p = jnp.exp2(s - m)
            denom = jnp.sum(p, axis=-1, keepdims=True) + jnp.exp2(sink - m)
            o = jnp.dot(p.astype(BF16), vals, preferred_element_type=F32)
            o_ref[:, head * HEAD_DIM:(head + 1) * HEAD_DIM] = (o / denom).astype(o_ref.dtype)


def _attn_prompt(qs, k, v, sinks):
    t_p = k.shape[0]
    dq = sum(q.shape[1] for q in qs)
    dk = k.shape[1]
    blk = WINDOW
    n_heads = dq // HEAD_DIM
    cur = lambda n: (n, 0)
    prev = lambda n: (jnp.maximum(n - 1, 0), 0)
    pipelined = 2 * _nbytes((blk, dq), BF16) + 4 * _nbytes((blk, dk), BF16)
    bias_shape = (2, n_heads, blk, 2 * blk)
    return pl.pallas_call(
        functools.partial(_attn_prompt_kernel, n_heads=n_heads, n_q=len(qs)),
        grid=(t_p // blk,),
        in_specs=[pl.BlockSpec(memory_space=pltpu.SMEM)]
        + [pl.BlockSpec((blk, q.shape[1]), cur) for q in qs]
        + [pl.BlockSpec((blk, dk), prev), pl.BlockSpec((blk, dk), cur),
           pl.BlockSpec((blk, dk), prev), pl.BlockSpec((blk, dk), cur)],
        out_specs=pl.BlockSpec((blk, dq), cur),
        out_shape=jax.ShapeDtypeStruct((t_p, dq), BF16),
        scratch_shapes=[pltpu.VMEM(bias_shape, F32)],
        compiler_params=_params(1, pipelined, _nbytes(bias_shape, F32)
                                + 16 * _nbytes((blk, 2 * blk), F32)),
        name="attn_prompt",
    )(sinks, *qs, k, k, v, v)


def _attn_sample_kernel(sink_ref, q_ref, kn_ref, vn_ref, ck_ref, cv_ref,
                        o_ref, ok_ref, ov_ref, *, n_heads, bb):
    group = n_heads // N_KV_HEADS
    rows = WINDOW * N_KV_HEADS
    col = lax.broadcasted_iota(jnp.int32, (n_heads, rows), 1)
    head = lax.broadcasted_iota(jnp.int32, (n_heads, rows), 0)
    own_kv = jnp.bitwise_and(col, N_KV_HEADS - 1) == _shift_div(head, group)
    dist = (WINDOW - 1 - _shift_div(col, N_KV_HEADS)).astype(F32)
    slope = jnp.exp2(-8.0 * (head + 1).astype(F32) / n_heads)
    bias = jnp.where(own_kv, slope * dist, jnp.inf)
    sink = sink_ref[...]
    scale = HEAD_DIM ** -0.5

    for b in range(bb):
        for c_ref, n_ref, out_ref in ((ck_ref, kn_ref, ok_ref), (cv_ref, vn_ref, ov_ref)):
            out_ref[b, 0:rows - N_KV_HEADS, :] = c_ref[b, N_KV_HEADS:rows, :]
            for kv in range(N_KV_HEADS):
                r = rows - N_KV_HEADS + kv
                out_ref[b, r:r + 1, :] = n_ref[b:b + 1, kv * HEAD_DIM:(kv + 1) * HEAD_DIM]
        keys = ok_ref[b].astype(BF16)
        vals = ov_ref[b].astype(BF16)
        s = lax.dot_general(q_ref[b], keys, (((1,), (1,)), ((), ())),
                            preferred_element_type=F32)
        p, denom = _softmax_sink(s * scale - bias, sink)
        o = jnp.dot(p.astype(BF16), vals, preferred_element_type=F32)
        o_ref[b] = (o / denom).astype(o_ref.dtype)


def _attn_sample(q, k_new, v_new, cache_k, cache_v, sink_col, *, bb):
    t_s, n_heads, _ = q.shape
    dk = N_KV_HEADS * HEAD_DIM
    rows = WINDOW * N_KV_HEADS
    cache = pl.BlockSpec((bb, rows, HEAD_DIM), lambda i: (i, 0, 0))
    new = pl.BlockSpec((bb, dk), lambda i: (i, 0))
    qspec = pl.BlockSpec((bb, n_heads, HEAD_DIM), lambda i: (i, 0, 0))
    pipelined = (4 * _nbytes((bb, rows, HEAD_DIM), F32) + 2 * _nbytes((bb, dk), F32)
                 + 2 * _nbytes((bb, n_heads, HEAD_DIM), BF16))
    return pl.pallas_call(
        functools.partial(_attn_sample_kernel, n_heads=n_heads, bb=bb),
        grid=(t_s // bb,),
        in_specs=[pl.BlockSpec((n_heads, 1), lambda i: (0, 0)), qspec, new, new,
                  cache, cache],
        out_specs=[qspec, cache, cache],
        out_shape=[jax.ShapeDtypeStruct((t_s, n_heads, HEAD_DIM), BF16),
                   jax.ShapeDtypeStruct((t_s, rows, HEAD_DIM), F32),
                   jax.ShapeDtypeStruct((t_s, rows, HEAD_DIM), F32)],
        compiler_params=_params(1, pipelined, 8 * _nbytes((WINDOW, dk), F32)),
        name="attn_sample",
    )(sink_col, q, k_new, v_new, cache_k, cache_v)


def _swiglu_ffn(x, h, rstd, w_gate, w_up, w_down, next_gain, *, tm, tm_down, tn):
    d_ff = w_gate.shape[1]
    (a,) = _matmul("ffn_gate_up", [h], [(w_gate, 0), (w_up, 0)], [], [BF16], _ep_swiglu,
                   n_cols=d_ff, tm=tm, tn=tn, rowscale=rstd)
    return _ffn_down(a, w_down, x, next_gain, tm=tm_down, tn=tn)


def kernel(x_prompt, x_sample, state_conv, state_h, cache_k, cache_v, norm_ffn1, w_ffn1_gate, w_ffn1_up, w_ffn1_down, norm_mix, w_in, conv_w, conv_b, rg_w_a, rg_b_a, rg_w_x, rg_b_x, rg_lambda, q_norm, k_norm, sinks, w_lru_proj, w_attn_proj, w_out, norm_ffn2, w_ffn2_gate, w_ffn2_up, w_ffn2_down):
    b_p, t_p, d = x_prompt.shape
    t_s = x_sample.shape[0]
    depth = w_in.shape[0]
    assert b_p == 1 and x_sample.shape[1] == 1 and depth == 1
    n_heads = d // HEAD_DIM
    dk = N_KV_HEADS * HEAD_DIM
    tm, tm_down, tn = 2048, 1024, V7X_MXU_DIM
    mm = functools.partial(_matmul, tm=tm, tn=tn)
    ffn = functools.partial(_swiglu_ffn, tm=tm, tm_down=tm_down, tn=tn)

    x = (x_prompt.reshape(t_p, d), x_sample.reshape(t_s, d))
    h = tuple(_rmsnorm(xi, norm_ffn1[0]) for xi in x)
    x, xb, rstd = ffn(x, h, None, w_ffn1_gate[0], w_ffn1_up[0], w_ffn1_down[0],
                      norm_mix[0].reshape(1, d))

    w = w_in[0]
    xr, gy = mm("in_rec", [xb], [(w, 0), (w, d)], [], [F32, F32], _ep_xr_gelu, n_cols=d,
                rowscale=rstd)
    q_lo, q_hi = mm("in_q", [xb], [(w, 2 * d), (w, 2 * d + d // 2)],
                    [("const", q_norm[0].reshape(1, HEAD_DIM))], [BF16, BF16],
                    _ep_head_rms2, n_cols=d // 2, rowscale=rstd)
    k32, kb, v32, vb = mm("in_kv", [xb], [(w, 3 * d), (w, 3 * d + dk)],
                          [("const", k_norm[0].reshape(1, HEAD_DIM))],
                          [F32, BF16, F32, BF16], _ep_kv, n_cols=dk, rowscale=rstd)
    sg_rec, sg_att = mm("in_gates", [xb], [(w, 3 * d + 2 * dk), (w, 4 * d + 2 * dk)], [],
                        [F32, F32], _ep_sigmoid2, n_cols=d, rowscale=rstd)

    cw, cb = conv_w[0], conv_b[0].reshape(1, d)
    wa, wx = rg_w_a[0].astype(BF16), rg_w_x[0].astype(BF16)
    ba, bx = rg_b_a[0].reshape(1, d), rg_b_x[0].reshape(1, d)
    lam = rg_lambda[0].reshape(1, d)
    hg_p, h_last = _lru_prompt(xr[0], gy[0], cw, cb, wa, wx, ba, bx, lam, tc=256)
    hg_s, h_s = _lru_sample(xr[1], gy[1], state_conv[0].reshape(t_s, (CONV_W - 1) * d),
                            state_h[0], cw, cb, wa, wx, ba, bx, lam)

    o_p = _attn_prompt([q_lo[0], q_hi[0]], kb[0], vb[0], sinks[0])
    q_s = jnp.concatenate([q_lo[1], q_hi[1]], axis=1)
    o_s, new_k, new_v = _attn_sample(
        q_s.reshape(t_s, n_heads, HEAD_DIM), k32[1], v32[1],
        cache_k[0].reshape(t_s, WINDOW * N_KV_HEADS, HEAD_DIM),
        cache_v[0].reshape(t_s, WINDOW * N_KV_HEADS, HEAD_DIM),
        sinks[0].reshape(n_heads, 1), bb=8)
    o = (o_p, o_s.reshape(t_s, d))

    mm1 = functools.partial(_matmul, tm=tm // 2, tn=2 * tn, n_cols=d)
    (y_rec,) = mm1("lru_proj", [(hg_p, hg_s)], [(w_lru_proj[0], 0)], [("pair",) + sg_rec],
                   [F32], _ep_gate_mul)
    (mixed,) = mm1("attn_proj", [o], [(w_attn_proj[0], 0)],
                   [("pair",) + sg_att, ("pair",) + y_rec], [BF16], _ep_gate_mul_add)
    x, xb, rstd = mm1("out_proj", [mixed], [(w_out[0], 0)], [("pair",) + x], [F32],
                      _ep_residual, norm_gain=norm_ffn2[0].reshape(1, d))

    (y,) = ffn(x, xb, rstd, w_ffn2_gate[0], w_ffn2_up[0], w_ffn2_down[0], None)

    kv_shape = (1, -1, WINDOW, N_KV_HEADS, HEAD_DIM)
    n_conv = CONV_W - 1
    sample_conv = jnp.concatenate([state_conv[0][:, 1:], xr[1][:, None, :]], axis=1)
    return (y[0].reshape(1, t_p, d), y[1].reshape(t_s, 1, d),
            xr[0][t_p - n_conv:].reshape(1, 1, n_conv, d), h_last.reshape(1, 1, d),
            k32[0][t_p - WINDOW:].reshape(kv_shape), v32[0][t_p - WINDOW:].reshape(kv_shape),
            sample_conv[None], h_s[None],
            new_k.reshape(kv_shape), new_v.reshape(kv_shape))
```

```python
import functools
import math

import jax
import jax.numpy as jnp
from jax import lax
from jax.experimental import pallas as pl
from jax.experimental.pallas import tpu as pltpu

HEAD_DIM = 128
N_KV_HEADS = 8
WINDOW = 128
N_RG_BLOCKS = 16
CONV_W = 4
RG_C = 8.0
NORM_EPS = 1e-6
MACARON_WEIGHT = 0.5

V7X_VMEM_BYTES = 64 * 1024 * 1024
V7X_LANES = 128
V7X_SUBLANES = 8
V7X_MXU_DIM = 256
V7X_VMEM_RESERVE_BYTES = 4 * 1024 * 1024
V7X_VMEM_MAX_REQUEST_BYTES = V7X_VMEM_BYTES - 1024 * 1024

F32 = jnp.float32
BF16 = jnp.bfloat16


def _nbytes(shape, dtype):
    return math.prod(shape) * jnp.dtype(dtype).itemsize


def _vmem_limit(pipelined_bytes, resident_bytes):
    want = 2 * pipelined_bytes + resident_bytes + V7X_VMEM_RESERVE_BYTES
    return int(min(want, V7X_VMEM_MAX_REQUEST_BYTES))


def _params(n_grid, pipelined_bytes, resident_bytes):
    return pltpu.CompilerParams(
        dimension_semantics=("arbitrary",) * n_grid,
        vmem_limit_bytes=_vmem_limit(pipelined_bytes, resident_bytes))


def _shift_div(x, n):
    assert n & (n - 1) == 0
    return lax.shift_right_logical(x, jnp.int32(n.bit_length() - 1))


def _rms_kernel(x_ref, g_ref, o_ref):
    x = x_ref[...]
    ms = jnp.mean(x * x, axis=-1, keepdims=True)
    o_ref[...] = (x * lax.rsqrt(ms + NORM_EPS) * g_ref[...]).astype(o_ref.dtype)


def _rmsnorm(x, g):
    rows, d = x.shape
    rb = min(rows, 256)
    assert rows % rb == 0
    blk = _nbytes((rb, d), F32) + _nbytes((rb, d), BF16)
    return pl.pallas_call(
        _rms_kernel,
        grid=(rows // rb,),
        in_specs=[pl.BlockSpec((rb, d), lambda i: (i, 0)),
                  pl.BlockSpec((1, d), lambda i: (0, 0))],
        out_specs=pl.BlockSpec((rb, d), lambda i: (i, 0)),
        out_shape=jax.ShapeDtypeStruct((rows, d), BF16),
        compiler_params=_params(1, blk, 3 * _nbytes((rb, d), F32)),
        name="rmsnorm",
    )(x, g.reshape(1, d))


def _tile_maps(n_i, n_j):
    def prompt(i, j, *_):
        return (i, j)

    def sample(i, j, *_):
        return (0, jnp.where(i == n_i - 1, j, 0))

    return prompt, sample


def _emit_norm(x, gain, xb_ref, rstd_ref, ss_ref, j, n_j, d):
    xb_ref[...] = (x * gain).astype(xb_ref.dtype)
    sq = x * x
    part = sq[:, 0:V7X_LANES]
    for t in range(1, x.shape[1] // V7X_LANES):
        part = part + sq[:, t * V7X_LANES:(t + 1) * V7X_LANES]

    @pl.when(j == 0)
    def _():
        ss_ref[...] = part

    @pl.when(j != 0)
    def _():
        ss_ref[...] += part

    @pl.when(j == n_j - 1)
    def _():
        ms = jnp.sum(ss_ref[...], axis=-1, keepdims=True) / d
        rstd_ref[...] = jnp.broadcast_to(lax.rsqrt(ms + NORM_EPS), rstd_ref.shape)


def _mm_kernel(*refs, n_x, n_w, extra_kinds, n_out, epilogue, n_i, n_j, rowscale, norm_d):
    pos = 0
    xs = [(refs[pos + 2 * t], refs[pos + 2 * t + 1]) for t in range(n_x)]
    pos += 2 * n_x
    ws = list(refs[pos:pos + n_w])
    pos += n_w
    extras = []
    for kind in extra_kinds:
        if kind == "pair":
            extras.append((refs[pos], refs[pos + 1]))
            pos += 2
        else:
            extras.append((refs[pos], refs[pos]))
            pos += 1
    if rowscale:
        scale_refs = (refs[pos], refs[pos + 1])
        pos += 2
    if norm_d:
        gain_ref = refs[pos]
        pos += 1
    outs = [(refs[pos + 2 * t], refs[pos + 2 * t + 1]) for t in range(n_out)]
    pos += 2 * n_out
    if norm_d:
        xb_refs, rstd_refs, ss_refs = (tuple(refs[pos + 2 * t:pos + 2 * t + 2])
                                       for t in range(3))
    i = pl.program_id(0)
    j = pl.program_id(1)
    wb = [w[...].astype(BF16) for w in ws]

    def compute(which):
        accs = []
        for t in range(n_w):
            x = xs[t if n_x > 1 else 0][which][...]
            accs.append(jnp.dot(x, wb[t], preferred_element_type=F32))
        if rowscale:
            r = scale_refs[which][...]
            scale = jnp.concatenate([r] * (accs[0].shape[1] // V7X_LANES), axis=1)
            accs = [acc * scale for acc in accs]
        res = epilogue(accs, [e[which][...] for e in extras])
        for o, r in zip(outs, res):
            o[which][...] = r.astype(o[which].dtype)
        if norm_d:
            _emit_norm(res[0], gain_ref[...], xb_refs[which], rstd_refs[which],
                       ss_refs[which], j, n_j, norm_d)

    compute(0)

    @pl.when(i == n_i - 1)
    def _():
        compute(1)


def _norm_side_specs(t_p, t_s, n_cols, tm, tn, map_p, map_s):
    rows_p = pl.BlockSpec((tm, V7X_LANES), lambda i, *_: (i, 0))
    rows_s = pl.BlockSpec((t_s, V7X_LANES), lambda i, *_: (0, 0))
    specs = [pl.BlockSpec((tm, tn), map_p), pl.BlockSpec((t_s, tn), map_s), rows_p, rows_s]
    shapes = [jax.ShapeDtypeStruct((t_p, n_cols), BF16),
              jax.ShapeDtypeStruct((t_s, n_cols), BF16),
              jax.ShapeDtypeStruct((t_p, V7X_LANES), F32),
              jax.ShapeDtypeStruct((t_s, V7X_LANES), F32)]
    scratch = [pltpu.VMEM((tm, V7X_LANES), F32), pltpu.VMEM((t_s, V7X_LANES), F32)]
    nbytes = _nbytes((tm + t_s, tn), BF16) + 2 * _nbytes((tm + t_s, V7X_LANES), F32)
    return specs, shapes, scratch, nbytes


def _matmul(name, xs, ws, extras, out_dtypes, epilogue, *, n_cols, tm, tn, rowscale=None,
            norm_gain=None):
    t_p = xs[0][0].shape[0]
    t_s = xs[0][1].shape[0]
    x_widths = [x_p.shape[1] for x_p, _ in xs]
    k = x_widths[0]
    assert len(xs) in (1, len(ws)) and all(kx == k for kx in x_widths)
    assert t_p % tm == 0 and n_cols % tn == 0
    n_i = t_p // tm
    n_j = n_cols // tn
    map_p, map_s = _tile_maps(n_i, n_j)

    in_specs, args = [], []
    pipelined = 0
    for w, col0 in ws:
        assert col0 % tn == 0 and w.shape[0] == k
        in_specs.append(pl.BlockSpec((k, tn), lambda i, j, c=col0 // tn: (0, j + c)))
        args.append(w)
        pipelined += _nbytes((k, tn), w.dtype)
    kinds = []
    for e in extras:
        kinds.append(e[0])
        if e[0] == "pair":
            in_specs += [pl.BlockSpec((tm, tn), map_p), pl.BlockSpec((t_s, tn), map_s)]
            args += [e[1], e[2]]
            pipelined += _nbytes((tm + t_s, tn), e[1].dtype)
        else:
            assert e[0] == "const" and e[1].ndim == 2
            in_specs.append(pl.BlockSpec(e[1].shape, lambda i, j: (0, 0)))
            args.append(e[1])
    if rowscale is not None:
        in_specs += [pl.BlockSpec((tm, V7X_LANES), lambda i, j: (i, 0)),
                     pl.BlockSpec((t_s, V7X_LANES), lambda i, j: (0, 0))]
        args += list(rowscale)
        pipelined += _nbytes((tm + t_s, V7X_LANES), F32)
    if norm_gain is not None:
        in_specs.append(pl.BlockSpec((1, tn), lambda i, j: (0, j)))
        args.append(norm_gain)
    out_specs, out_shape, scratch = [], [], []
    for dt in out_dtypes:
        out_specs += [pl.BlockSpec((tm, tn), map_p), pl.BlockSpec((t_s, tn), map_s)]
        out_shape += [jax.ShapeDtypeStruct((t_p, n_cols), dt),
                      jax.ShapeDtypeStruct((t_s, n_cols), dt)]
        pipelined += _nbytes((tm + t_s, tn), dt)
    if norm_gain is not None:
        assert n_cols == norm_gain.shape[1]
        side = _norm_side_specs(t_p, t_s, n_cols, tm, tn, map_p, map_s)
        out_specs += side[0]
        out_shape += side[1]
        scratch = side[2]
        pipelined += side[3]
    resident = len(ws) * (_nbytes((k, tn), BF16) + 4 * _nbytes((tm, tn), F32))
    pipelined += _nbytes((t_s, sum(x_widths)), BF16)
    x_bytes = _nbytes((tm, sum(x_widths)), BF16)
    double_x = (2 * (pipelined + x_bytes) + resident + V7X_VMEM_RESERVE_BYTES
                <= V7X_VMEM_MAX_REQUEST_BYTES)
    if double_x:
        pipelined += x_bytes
    else:
        resident += x_bytes
    x_specs, x_args = [], []
    for (x_p, x_s), kx in zip(xs, x_widths):
        assert x_p.dtype == BF16 and x_s.dtype == BF16
        if double_x:
            x_spec = pl.BlockSpec((tm, kx), lambda i, j: (i, 0))
        else:
            x_spec = pl.BlockSpec((tm, kx), lambda i, j: (i, 0),
                                  pipeline_mode=pl.Buffered(1))
        x_specs += [x_spec, pl.BlockSpec((t_s, kx), lambda i, j: (0, 0))]
        x_args += [x_p, x_s]
    in_specs = x_specs + in_specs
    args = x_args + args
    res = pl.pallas_call(
        functools.partial(_mm_kernel, n_x=len(xs), n_w=len(ws), extra_kinds=tuple(kinds),
                          n_out=len(out_dtypes), epilogue=epilogue, n_i=n_i, n_j=n_j,
                          rowscale=rowscale is not None,
                          norm_d=n_cols if norm_gain is not None else 0),
        grid=(n_i, n_j),
        in_specs=in_specs,
        out_specs=out_specs,
        out_shape=out_shape,
        scratch_shapes=scratch,
        compiler_params=_params(2, pipelined, resident),
        name=name,
    )(*args)
    return [(res[2 * t], res[2 * t + 1]) for t in range(len(res) // 2)]


def _sigmoid(x):
    return 1.0 / (1.0 + jnp.exp(-x))


def _ep_swiglu(accs, extras):
    g, u = accs
    return [g * _sigmoid(g) * u]


def _ep_xr_gelu(accs, extras):
    x = accs[1]
    c = math.sqrt(2.0 / math.pi)
    return [accs[0], 0.5 * x * (1.0 + jnp.tanh(c * (x + 0.044715 * (x * x * x))))]


def _ep_sigmoid2(accs, extras):
    return [_sigmoid(accs[0]), _sigmoid(accs[1])]


def _head_rms(acc, gain):
    parts = []
    for h in range(acc.shape[1] // HEAD_DIM):
        y = acc[:, h * HEAD_DIM:(h + 1) * HEAD_DIM]
        ms = jnp.mean(y * y, axis=-1, keepdims=True)
        parts.append(y * lax.rsqrt(ms + NORM_EPS) * gain)
    return jnp.concatenate(parts, axis=1)


def _ep_head_rms2(accs, extras):
    return [_head_rms(accs[0], extras[0]), _head_rms(accs[1], extras[0])]


def _ep_kv(accs, extras):
    k = _head_rms(accs[0], extras[0])
    return [k, k, accs[1], accs[1]]


def _ep_gate_mul(accs, extras):
    return [extras[0] * accs[0]]


def _ep_gate_mul_add(accs, extras):
    return [extras[1] + extras[0] * accs[0]]


def _ep_residual(accs, extras):
    return [extras[0] + accs[0]]


def _down_kernel(*refs, n_i, n_j, k_half, norm_d):
    a_p, a_s, w_ref, r_p, r_s = refs[:5]
    pos = 5
    if norm_d:
        gain_ref = refs[pos]
        pos += 1
    o_p, o_s = refs[pos:pos + 2]
    if norm_d:
        xb_refs, rstd_refs, ss_refs = (tuple(refs[pos + 2 + 2 * t:pos + 4 + 2 * t])
                                       for t in range(3))
    i = pl.program_id(0)
    j = pl.program_id(1)
    kk = pl.program_id(2)
    w = w_ref[...].astype(BF16)
    k0 = pl.multiple_of(kk * k_half, V7X_LANES)

    def compute(which, a_ref, r_ref, o_ref):
        d = MACARON_WEIGHT * jnp.dot(a_ref[:, pl.ds(k0, k_half)], w,
                                     preferred_element_type=F32)

        @pl.when(kk == 0)
        def _():
            o_ref[...] = r_ref[...] + d

        @pl.when(kk != 0)
        def _():
            o = o_ref[...] + d
            o_ref[...] = o
            if norm_d:
                _emit_norm(o, gain_ref[...], xb_refs[which], rstd_refs[which],
                           ss_refs[which], j, n_j, norm_d)

    compute(0, a_p, r_p, o_p)

    @pl.when(i == n_i - 1)
    def _():
        compute(1, a_s, r_s, o_s)


def _ffn_down(a, w, res, norm_gain, *, tm, tn):
    a_p, a_s = a
    r_p, r_s = res
    t_p, k = a_p.shape
    t_s = a_s.shape[0]
    n_cols = w.shape[1]
    assert k % 2 == 0 and (k // 2) % V7X_LANES == 0
    k_half = k // 2
    n_i = t_p // tm
    n_j = n_cols // tn
    map_p, map_s = _tile_maps(n_i, n_j)
    tile_p = pl.BlockSpec((tm, tn), map_p)
    tile_s = pl.BlockSpec((t_s, tn), map_s)
    pipelined = (_nbytes((k_half, tn), F32) + 2 * _nbytes((tm + t_s, tn), F32)
                 + _nbytes((t_s, k), BF16))
    resident = (_nbytes((tm, k), BF16) + _nbytes((k_half, tn), BF16)
                + 2 * _nbytes((tm, tn), F32))
    in_specs = [
        pl.BlockSpec((tm, k), lambda i, j, kk: (i, 0), pipeline_mode=pl.Buffered(1)),
        pl.BlockSpec((t_s, k), lambda i, j, kk: (0, 0)),
        pl.BlockSpec((k_half, tn), lambda i, j, kk: (kk, j)),
        tile_p, tile_s]
    args = [a_p, a_s, w, r_p, r_s]
    out_specs = [tile_p, tile_s]
    out_shape = [jax.ShapeDtypeStruct((t_p, n_cols), F32),
                 jax.ShapeDtypeStruct((t_s, n_cols), F32)]
    scratch = []
    if norm_gain is not None:
        in_specs.append(pl.BlockSpec((1, tn), lambda i, j, kk: (0, j)))
        args.append(norm_gain)
        side = _norm_side_specs(t_p, t_s, n_cols, tm, tn, map_p, map_s)
        out_specs += side[0]
        out_shape += side[1]
        scratch = side[2]
        pipelined += side[3]
    res = pl.pallas_call(
        functools.partial(_down_kernel, n_i=n_i, n_j=n_j, k_half=k_half,
                          norm_d=n_cols if norm_gain is not None else 0),
        grid=(n_i, n_j, 2),
        in_specs=in_specs,
        out_specs=out_specs,
        out_shape=out_shape,
        scratch_shapes=scratch,
        compiler_params=_params(3, pipelined, resident),
        name="ffn_down",
    )(*args)
    return [(res[2 * t], res[2 * t + 1]) for t in range(len(res) // 2)]


def _log_sigmoid(x):
    return jnp.minimum(x, 0.0) - jnp.log1p(jnp.exp(-jnp.abs(x)))


def _rg_gate_block(xc, wa, wx, ba, bx, lam, first_pos):
    xb = xc.astype(BF16)
    r = _sigmoid(jnp.dot(xb, wa, preferred_element_type=F32) + ba)
    g = _sigmoid(jnp.dot(xb, wx, preferred_element_type=F32) + bx)
    a = jnp.exp(r * (RG_C * _log_sigmoid(lam)))
    v = 1.0 - a * a
    mult = jnp.where(v > 0.0, v * lax.rsqrt(v), 0.0)
    if first_pos is not None:
        mult = jnp.where(first_pos, 1.0, mult)
    return a, mult * g * xc


def _lru_prompt_kernel(xr_ref, gy_ref, cw_ref, cb_ref, wa_ref, wx_ref, ba_ref, bx_ref,
                       lam_ref, hg_ref, hlast_ref, x_tail, a_buf, b_buf, h_carry, *, tc):
    t = pl.program_id(0)
    d = xr_ref.shape[1]
    bw = d // N_RG_BLOCKS
    pad = V7X_SUBLANES

    @pl.when(t == 0)
    def _():
        x_tail[...] = jnp.zeros_like(x_tail)
        h_carry[...] = jnp.zeros_like(h_carry)

    row = lax.broadcasted_iota(jnp.int32, (tc, 1), 0)
    first_pos = (row + t * tc) == 0
    n_groups = tc // V7X_SUBLANES
    slabs_per_block = bw // V7X_LANES
    for c in range(N_RG_BLOCKS):
        sl = slice(c * bw, (c + 1) * bw)
        x = xr_ref[:, sl]
        x_ext = jnp.concatenate([x_tail[:, sl], x], axis=0)
        x_tail[:, sl] = x[tc - pad:, :]
        shifted = [x_ext]
        for _ in range(CONV_W - 1):
            shifted.append(pltpu.roll(shifted[-1], 1, axis=0))
        xc = cb_ref[:, sl]
        for j in range(CONV_W):
            back = CONV_W - 1 - j
            xs = x if back == 0 else shifted[back][pad:, :]
            xc = xc + cw_ref[j:j + 1, sl] * xs
        a, b = _rg_gate_block(xc, wa_ref[c], wx_ref[c], ba_ref[:, sl], bx_ref[:, sl],
                              lam_ref[:, sl], first_pos)
        for half in range(slabs_per_block):
            s = c * slabs_per_block + half
            lanes = slice(half * V7X_LANES, (half + 1) * V7X_LANES)
            a_buf[s] = a[:, lanes]
            b_buf[s] = b[:, lanes]
            a_run = a_buf[s, pl.ds(0, n_groups, stride=V7X_SUBLANES), :]
            b_run = b_buf[s, pl.ds(0, n_groups, stride=V7X_SUBLANES), :]
            for r in range(1, V7X_SUBLANES):
                rows_r = pl.ds(r, n_groups, stride=V7X_SUBLANES)
                a_r = a_buf[s, rows_r, :]
                b_run = a_r * b_run + b_buf[s, rows_r, :]
                a_run = a_r * a_run
                a_buf[s, rows_r, :] = a_run
                b_buf[s, rows_r, :] = b_run

    def group(gi, h_prev):
        rows = pl.ds(pl.multiple_of(gi * V7X_SUBLANES, V7X_SUBLANES), V7X_SUBLANES)
        h = b_buf[:, rows, :] + a_buf[:, rows, :] * h_prev
        b_buf[:, rows, :] = h
        return jnp.broadcast_to(h[:, V7X_SUBLANES - 1:V7X_SUBLANES, :], h.shape)

    h_last = lax.fori_loop(0, n_groups, group, h_carry[...], unroll=2)
    h_carry[...] = h_last
    for s in range(d // V7X_LANES):
        lanes = slice(s * V7X_LANES, (s + 1) * V7X_LANES)
        hlast_ref[:, lanes] = h_last[s, 0:1, :]
        hg_ref[:, lanes] = (b_buf[s] * gy_ref[:, lanes]).astype(hg_ref.dtype)


def _lru_prompt(xr, gy, conv_w, conv_b, wa, wx, ba, bx, lam, *, tc):
    t_p, d = xr.shape
    bw = d // N_RG_BLOCKS
    row = pl.BlockSpec((1, d), lambda t: (0, 0))
    wspec = pl.BlockSpec((N_RG_BLOCKS, bw, bw), lambda t: (0, 0, 0))
    chunk = pl.BlockSpec((tc, d), lambda t: (t, 0))
    pipelined = (2 * _nbytes((tc, d), F32) + _nbytes((tc, d), BF16)
                 + 2 * _nbytes((N_RG_BLOCKS, bw, bw), BF16))
    n_slabs = d // V7X_LANES
    scratch = [pltpu.VMEM((V7X_SUBLANES, d), F32), pltpu.VMEM((n_slabs, tc, V7X_LANES), F32),
               pltpu.VMEM((n_slabs, tc, V7X_LANES), F32),
               pltpu.VMEM((n_slabs, V7X_SUBLANES, V7X_LANES), F32)]
    resident = 4 * _nbytes((tc, d), F32)
    return pl.pallas_call(
        functools.partial(_lru_prompt_kernel, tc=tc),
        grid=(t_p // tc,),
        in_specs=[chunk, chunk, pl.BlockSpec((CONV_W, d), lambda t: (0, 0)), row,
                  wspec, wspec, row, row, row],
        out_specs=[chunk, row],
        out_shape=[jax.ShapeDtypeStruct((t_p, d), BF16),
                   jax.ShapeDtypeStruct((1, d), F32)],
        scratch_shapes=scratch,
        compiler_params=_params(1, pipelined, resident),
        name="lru_prompt",
    )(xr, gy, conv_w, conv_b, wa, wx, ba, bx, lam)


def _lru_sample_kernel(xr_ref, gy_ref, s0_ref, s1_ref, s2_ref, h0_ref, cw_ref, cb_ref,
                       wa_ref, wx_ref, ba_ref, bx_ref, lam_ref, hg_ref, h_ref):
    xc = cb_ref[...]
    for j, s_ref in enumerate((s0_ref, s1_ref, s2_ref)):
        xc = xc + cw_ref[j:j + 1, :] * s_ref[...]
    xc = xc + cw_ref[CONV_W - 1:CONV_W, :] * xr_ref[...]
    a, u = _rg_gate_block(xc, wa_ref[0], wx_ref[0], ba_ref[...], bx_ref[...],
                          lam_ref[...], None)
    h = u + a * h0_ref[...]
    h_ref[...] = h
    hg_ref[...] = (h * gy_ref[...]).astype(hg_ref.dtype)


def _lru_sample(xr, gy, state_conv, state_h, conv_w, conv_b, wa, wx, ba, bx, lam):
    t_s, d = xr.shape
    bw = d // N_RG_BLOCKS
    assert CONV_W == 4
    tile = pl.BlockSpec((t_s, bw), lambda c: (0, c))
    row = pl.BlockSpec((1, bw), lambda c: (0, c))
    wspec = pl.BlockSpec((1, bw, bw), lambda c: (c, 0, 0))
    state = [pl.BlockSpec((t_s, bw), lambda c, j=j: (0, j * N_RG_BLOCKS + c))
             for j in range(CONV_W - 1)]
    pipelined = 9 * _nbytes((t_s, bw), F32) + 2 * _nbytes((bw, bw), BF16)
    return pl.pallas_call(
        _lru_sample_kernel,
        grid=(N_RG_BLOCKS,),
        in_specs=[tile, tile] + state + [tile, pl.BlockSpec((CONV_W, bw), lambda c: (0, c)),
                                         row, wspec, wspec, row, row, row],
        out_specs=[tile, tile],
        out_shape=[jax.ShapeDtypeStruct((t_s, d), BF16),
                   jax.ShapeDtypeStruct((t_s, d), F32)],
        compiler_params=_params(1, pipelined, 8 * _nbytes((t_s, bw), F32)),
        name="lru_sample",
    )(xr, gy, state_conv, state_conv, state_conv, state_h, conv_w, conv_b, wa, wx,
      ba, bx, lam)


def _alibi_slope(head, n_heads):
    return 2.0 ** (-8.0 * (head + 1) / n_heads)


def _softmax_sink(s, sink):
    m = jnp.maximum(jnp.max(s, axis=-1, keepdims=True), sink)
    p = jnp.exp(s - m)
    return p, jnp.sum(p, axis=-1, keepdims=True) + jnp.exp(sink - m)


def _attn_prompt_kernel(*refs, n_heads, n_q):
    sinks_ref = refs[0]
    q_refs = refs[1:1 + n_q]
    kp_ref, kc_ref, vp_ref, vc_ref, o_ref, bias_ref = refs[1 + n_q:]
    n = pl.program_id(0)
    blk = o_ref.shape[0]
    group = n_heads // N_KV_HEADS
    heads_per_q = n_heads // n_q
    log2e = math.log2(math.e)

    @pl.when(n == 0)
    def _():
        qi = lax.broadcasted_iota(jnp.int32, (blk, 2 * blk), 0)
        kj = lax.broadcasted_iota(jnp.int32, (blk, 2 * blk), 1)
        dist = qi + blk - kj
        in_band = (dist >= 0) & (dist < WINDOW)
        dist_f = dist.astype(F32)
        for head in range(n_heads):
            term = jnp.where(in_band, (_alibi_slope(head, n_heads) * log2e) * dist_f, jnp.inf)
            bias_ref[1, head] = term
            bias_ref[0, head] = jnp.where(kj >= blk, term, jnp.inf)

    which = jnp.minimum(n, 1)
    scale = HEAD_DIM ** -0.5 * log2e
    for kv in range(N_KV_HEADS):
        cols = slice(kv * HEAD_DIM, (kv + 1) * HEAD_DIM)
        keys = jnp.concatenate([kp_ref[:, cols], kc_ref[:, cols]], axis=0)
        vals = jnp.concatenate([vp_ref[:, cols], vc_ref[:, cols]], axis=0)
        for g in range(group):
            head = kv * group + g
            q_ref = q_refs[head // heads_per_q]
            local = head % heads_per_q
            q = q_ref[:, local * HEAD_DIM:(local + 1) * HEAD_DIM]
            s = lax.dot_general(q, keys, (((1,), (1,)), ((), ())),
                                preferred_element_type=F32)
            s = s * scale - bias_ref[which, head]
            sink = jnp.full((blk, 1), sinks_ref[head] * log2e, F32)
            m = jnp.maximum(jnp.max(s, axis=-1, keepdims=True), sink)
            p = jnp.exp2(s - m)
            denom = jnp.sum(p, axis=-1, keepdims=True) + jnp.exp2(sink - m)
            o = jnp.dot(p.astype(BF16), vals, preferred_element_type=F32)
            o_ref[:, head * HEAD_DIM:(head + 1) * HEAD_DIM] = (o / denom).astype(o_ref.dtype)


def _attn_prompt(qs, k, v, sinks):
    t_p = k.shape[0]
    dq = sum(q.shape[1] for q in qs)
    dk = k.shape[1]
    blk = WINDOW
    n_heads = dq // HEAD_DIM
    cur = lambda n: (n, 0)
    prev = lambda n: (jnp.maximum(n - 1, 0), 0)
    pipelined = 2 * _nbytes((blk, dq), BF16) + 4 * _nbytes((blk, dk), BF16)
    bias_shape = (2, n_heads, blk, 2 * blk)
    return pl.pallas_call(
        functools.partial(_attn_prompt_kernel, n_heads=n_heads, n_q=len(qs)),
        grid=(t_p // blk,),
        in_specs=[pl.BlockSpec(memory_space=pltpu.SMEM)]
        + [pl.BlockSpec((blk, q.shape[1]), cur) for q in qs]
        + [pl.BlockSpec((blk, dk), prev), pl.BlockSpec((blk, dk), cur),
           pl.BlockSpec((blk, dk), prev), pl.BlockSpec((blk, dk), cur)],
        out_specs=pl.BlockSpec((blk, dq), cur),
        out_shape=jax.ShapeDtypeStruct((t_p, dq), BF16),
        scratch_shapes=[pltpu.VMEM(bias_shape, F32)],
        compiler_params=_params(1, pipelined, _nbytes(bias_shape, F32)
                                + 16 * _nbytes((blk, 2 * blk), F32)),
        name="attn_prompt",
    )(sinks, *qs, k, k, v, v)


def _attn_sample_kernel(sink_ref, q_ref, kn_ref, vn_ref, ck_ref, cv_ref,
                        o_ref, ok_ref, ov_ref, *, n_heads, bb):
    group = n_heads // N_KV_HEADS
    rows = WINDOW * N_KV_HEADS
    col = lax.broadcasted_iota(jnp.int32, (n_heads, rows), 1)
    head = lax.broadcasted_iota(jnp.int32, (n_heads, rows), 0)
    own_kv = jnp.bitwise_and(col, N_KV_HEADS - 1) == _shift_div(head, group)
    dist = (WINDOW - 1 - _shift_div(col, N_KV_HEADS)).astype(F32)
    slope = jnp.exp2(-8.0 * (head + 1).astype(F32) / n_heads)
    bias = jnp.where(own_kv, slope * dist, jnp.inf)
    sink = sink_ref[...]
    scale = HEAD_DIM ** -0.5

    for b in range(bb):
        for c_ref, n_ref, out_ref in ((ck_ref, kn_ref, ok_ref), (cv_ref, vn_ref, ov_ref)):
            out_ref[b, 0:rows - N_KV_HEADS, :] = c_ref[b, N_KV_HEADS:rows, :]
            for kv in range(N_KV_HEADS):
                r = rows - N_KV_HEADS + kv
                out_ref[b, r:r + 1, :] = n_ref[b:b + 1, kv * HEAD_DIM:(kv + 1) * HEAD_DIM]
        keys = ok_ref[b].astype(BF16)
        vals = ov_ref[b].astype(BF16)
        s = lax.dot_general(q_ref[b], keys, (((1,), (1,)), ((), ())),
                            preferred_element_type=F32)
        p, denom = _softmax_sink(s * scale - bias, sink)
        o = jnp.dot(p.astype(BF16), vals, preferred_element_type=F32)
        o_ref[b] = (o / denom).astype(o_ref.dtype)


def _attn_sample(q, k_new, v_new, cache_k, cache_v, sink_col, *, bb):
    t_s, n_heads, _ = q.shape
    dk = N_KV_HEADS * HEAD_DIM
    rows = WINDOW * N_KV_HEADS
    cache = pl.BlockSpec((bb, rows, HEAD_DIM), lambda i: (i, 0, 0))
    new = pl.BlockSpec((bb, dk), lambda i: (i, 0))
    qspec = pl.BlockSpec((bb, n_heads, HEAD_DIM), lambda i: (i, 0, 0))
    pipelined = (4 * _nbytes((bb, rows, HEAD_DIM), F32) + 2 * _nbytes((bb, dk), F32)
                 + 2 * _nbytes((bb, n_heads, HEAD_DIM), BF16))
    return pl.pallas_call(
        functools.partial(_attn_sample_kernel, n_heads=n_heads, bb=bb),
        grid=(t_s // bb,),
        in_specs=[pl.BlockSpec((n_heads, 1), lambda i: (0, 0)), qspec, new, new,
                  cache, cache],
        out_specs=[qspec, cache, cache],
        out_shape=[jax.ShapeDtypeStruct((t_s, n_heads, HEAD_DIM), BF16),
                   jax.ShapeDtypeStruct((t_s, rows, HEAD_DIM), F32),
                   jax.ShapeDtypeStruct((t_s, rows, HEAD_DIM), F32)],
        compiler_params=_params(1, pipelined, 8 * _nbytes((WINDOW, dk), F32)),
        name="attn_sample",
    )(sink_col, q, k_new, v_new, cache_k, cache_v)


def _swiglu_ffn(x, h, rstd, w_gate, w_up, w_down, next_gain, *, tm, tm_down, tn):
    d_ff = w_gate.shape[1]
    (a,) = _matmul("ffn_gate_up", [h], [(w_gate, 0), (w_up, 0)], [], [BF16], _ep_swiglu,
                   n_cols=d_ff, tm=tm, tn=tn, rowscale=rstd)
    return _ffn_down(a, w_down, x, next_gain, tm=tm_down, tn=tn)


def kernel(x_prompt, x_sample, state_conv, state_h, cache_k, cache_v, norm_ffn1, w_ffn1_gate, w_ffn1_up, w_ffn1_down, norm_mix, w_in, conv_w, conv_b, rg_w_a, rg_b_a, rg_w_x, rg_b_x, rg_lambda, q_norm, k_norm, sinks, w_lru_proj, w_attn_proj, w_out, norm_ffn2, w_ffn2_gate, w_ffn2_up, w_ffn2_down):
    b_p, t_p, d = x_prompt.shape
    t_s = x_sample.shape[0]
    depth = w_in.shape[0]
    assert b_p == 1 and x_sample.shape[1] == 1 and depth == 1
    n_heads = d // HEAD_DIM
    dk = N_KV_HEADS * HEAD_DIM
    tm, tm_down, tn = 2048, 1024, V7X_MXU_DIM
    mm = functools.partial(_matmul, tm=tm, tn=tn)
    ffn = functools.partial(_swiglu_ffn, tm=tm, tm_down=tm_down, tn=tn)

    x = (x_prompt.reshape(t_p, d), x_sample.reshape(t_s, d))
    h = tuple(_rmsnorm(xi, norm_ffn1[0]) for xi in x)
    x, xb, rstd = ffn(x, h, None, w_ffn1_gate[0], w_ffn1_up[0], w_ffn1_down[0],
                      norm_mix[0].reshape(1, d))

    w = w_in[0]
    xr, gy = mm("in_rec", [xb], [(w, 0), (w, d)], [], [F32, F32], _ep_xr_gelu, n_cols=d,
                rowscale=rstd)
    q_lo, q_hi = mm("in_q", [xb], [(w, 2 * d), (w, 2 * d + d // 2)],
                    [("const", q_norm[0].reshape(1, HEAD_DIM))], [BF16, BF16],
                    _ep_head_rms2, n_cols=d // 2, rowscale=rstd)
    k32, kb, v32, vb = _matmul("in_kv", [xb], [(w, 3 * d), (w, 3 * d + dk)],
                               [("const", k_norm[0].reshape(1, HEAD_DIM))],
                               [F32, BF16, F32, BF16], _ep_kv, n_cols=dk, tm=tm // 2, tn=tn,
                               rowscale=rstd)
    sg_rec, sg_att = mm("in_gates", [xb], [(w, 3 * d + 2 * dk), (w, 4 * d + 2 * dk)], [],
                        [F32, F32], _ep_sigmoid2, n_cols=d, rowscale=rstd)

    cw, cb = conv_w[0], conv_b[0].reshape(1, d)
    wa, wx = rg_w_a[0].astype(BF16), rg_w_x[0].astype(BF16)
    ba, bx = rg_b_a[0].reshape(1, d), rg_b_x[0].reshape(1, d)
    lam = rg_lambda[0].reshape(1, d)
    hg_p, h_last = _lru_prompt(xr[0], gy[0], cw, cb, wa, wx, ba, bx, lam, tc=256)
    hg_s, h_s = _lru_sample(xr[1], gy[1], state_conv[0].reshape(t_s, (CONV_W - 1) * d),
                            state_h[0], cw, cb, wa, wx, ba, bx, lam)

    o_p = _attn_prompt([q_lo[0], q_hi[0]], kb[0], vb[0], sinks[0])
    q_s = jnp.concatenate([q_lo[1], q_hi[1]], axis=1)
    o_s, new_k, new_v = _attn_sample(
        q_s.reshape(t_s, n_heads, HEAD_DIM), k32[1], v32[1],
        cache_k[0].reshape(t_s, WINDOW * N_KV_HEADS, HEAD_DIM),
        cache_v[0].reshape(t_s, WINDOW * N_KV_HEADS, HEAD_DIM),
        sinks[0].reshape(n_heads, 1), bb=8)
    o = (o_p, o_s.reshape(t_s, d))

    mm1 = functools.partial(_matmul, tm=tm // 2, tn=2 * tn, n_cols=d)
    (y_rec,) = mm1("lru_proj", [(hg_p, hg_s)], [(w_lru_proj[0], 0)], [("pair",) + sg_rec],
                   [F32], _ep_gate_mul)
    (mixed,) = mm1("attn_proj", [o], [(w_attn_proj[0], 0)],
                   [("pair",) + sg_att, ("pair",) + y_rec], [BF16], _ep_gate_mul_add)
    x, xb, rstd = mm1("out_proj", [mixed], [(w_out[0], 0)], [("pair",) + x], [F32],
                      _ep_residual, norm_gain=norm_ffn2[0].reshape(1, d))

    (y,) = ffn(x, xb, rstd, w_ffn2_gate[0], w_ffn2_up[0], w_ffn2_down[0], None)

    kv_shape = (1, -1, WINDOW, N_KV_HEADS, HEAD_DIM)
    n_conv = CONV_W - 1
    sample_conv = jnp.concatenate([state_conv[0][:, 1:], xr[1][:, None, :]], axis=1)
    return (y[0].reshape(1, t_p, d), y[1].reshape(t_s, 1, d),
            xr[0][t_p - n_conv:].reshape(1, 1, n_conv, d), h_last.reshape(1, 1, d),
            k32[0][t_p - WINDOW:].reshape(kv_shape), v32[0][t_p - WINDOW:].reshape(kv_shape),
            sample_conv[None], h_s[None],
            new_k.reshape(kv_shape), new_v.reshape(kv_shape))
```
